```python
import math
import jax, jax.numpy as jnp
from jax import lax
import numpy as np

D_MODEL = 1024
BATCH = 8
SEQ = 2048
DEPTH = 2

HEAD_DIM = 64
ATTN_WIDTH = D_MODEL // 2
ATTN_HEADS = ATTN_WIDTH // HEAD_DIM
KV_HEADS = 2
GROUP = ATTN_HEADS // KV_HEADS
IDX_HEADS = 16
IDX_DIM = 64
INDEX_TOPK = 256
Q_BLOCK = 128
CONV_WIDTH = D_MODEL // 4
CONV_KERNEL = 31
POOL_WIDTH = D_MODEL // 4
POOL_GROUPS = 4
POOL_GROUP_DIM = POOL_WIDTH // POOL_GROUPS
POOL_WINDOWS = (2, 4, 8, 16)
POOL_PAD = max(POOL_WINDOWS)
MIX_WIDTH = ATTN_WIDTH + CONV_WIDTH + POOL_WIDTH
D_FF = 4 * D_MODEL
PLE_DIM = 256
ROPE_THETA = 500000.0
ROPE_DIM = HEAD_DIM // 4
MAX_POS_OFFSET = 1024
NORM_EPS = 1e-6
LN_EPS = 1e-5

IN_SIZES = (
    ATTN_WIDTH,
    KV_HEADS * HEAD_DIM,
    KV_HEADS * HEAD_DIM,
    IDX_HEADS * IDX_DIM,
    IDX_DIM,
    IDX_HEADS,
    2 * CONV_WIDTH,
    POOL_WIDTH,
)
IN_WIDTH = int(sum(IN_SIZES))
IN_SPLITS = [int(v) for v in np.cumsum(IN_SIZES)[:-1]]

kernel_name = "hybrid_dsa_conformer_pool_block"


def rmsnorm(x, g):
    xf = x.astype(jnp.float32)
    y = xf * lax.rsqrt(jnp.mean(xf * xf, axis=-1, keepdims=True) + NORM_EPS)
    return (y * g.astype(jnp.float32)).astype(x.dtype)


def layernorm(x, g, b):
    xf = x.astype(jnp.float32)
    mu = jnp.mean(xf, axis=-1, keepdims=True)
    var = jnp.mean(jnp.square(xf - mu), axis=-1, keepdims=True)
    y = (xf - mu) * lax.rsqrt(var + LN_EPS)
    return (y * g.astype(jnp.float32) + b.astype(jnp.float32)).astype(x.dtype)


def rope_tables(positions):
    inv_freq = ROPE_THETA ** (-jnp.arange(0, ROPE_DIM, 2, dtype=jnp.float32) / ROPE_DIM)
    ang = positions.astype(jnp.float32)[..., None] * inv_freq
    return jnp.cos(ang)[:, :, None, :], jnp.sin(ang)[:, :, None, :]


def partial_rope(x, cos, sin):
    half = ROPE_DIM // 2
    cos = cos.astype(x.dtype)
    sin = sin.astype(x.dtype)
    x1 = x[..., :half]
    x2 = x[..., half:ROPE_DIM]
    return jnp.concatenate([x1 * cos - x2 * sin, x2 * cos + x1 * sin, x[..., ROPE_DIM:]], axis=-1)


def dsa_sparse_attention(q, k, v, qi, ki, wi):
    B, S = q.shape[0], q.shape[1]
    n_blk = S // Q_BLOCK
    k_sel = min(INDEX_TOPK, S // 4)
    idx_scale = (IDX_DIM ** -0.5) * (IDX_HEADS ** -0.5)
    att_scale = HEAD_DIM ** -0.5
    neg = jnp.finfo(jnp.float32).min
    key_pos = jnp.arange(S)
    b_idx = jnp.arange(B)[:, None, None]
    ki_f = ki.astype(jnp.float32)

    def to_blocks(t):
        return jnp.moveaxis(t.reshape((B, n_blk, Q_BLOCK) + t.shape[2:]), 1, 0)

    q_b = to_blocks(q.reshape(B, S, KV_HEADS, GROUP, HEAD_DIM))
    qi_b = to_blocks(qi)
    wi_b = to_blocks(wi)
    t_b = key_pos.reshape(n_blk, Q_BLOCK)

    def block(args):
        q_blk, qi_blk, wi_blk, t_blk = args
        dots = jnp.einsum('bthd,bsd->bths', qi_blk.astype(jnp.float32), ki_f)
        idx = jnp.einsum('bths,bth->bts', jax.nn.relu(dots), wi_blk.astype(jnp.float32)) * idx_scale
        causal = key_pos[None, :] <= t_blk[:, None]
        idx = jnp.where(causal[None], idx, -jnp.inf)
        _, sel = lax.top_k(idx, k_sel)
        valid = sel <= t_blk[None, :, None]
        k_g = k[b_idx, sel]
        v_g = v[b_idx, sel]
        s = jnp.einsum('bthgd,btnhd->bthgn', q_blk, k_g).astype(jnp.float32) * att_scale
        s = jnp.where(valid[:, :, None, None, :], s, neg)
        pr = jax.nn.softmax(s, axis=-1).astype(v.dtype)
        o = jnp.einsum('bthgn,btnhd->bthgd', pr, v_g)
        return o.reshape(B, Q_BLOCK, ATTN_WIDTH)

    out = lax.map(block, (q_b, qi_b, wi_b, t_b))
    return jnp.moveaxis(out, 0, 1).reshape(B, S, ATTN_WIDTH)


def conformer_conv(u, conv_dw, conv_b, ln_g, ln_b, conv_pw):
    a, g = jnp.split(u, 2, axis=-1)
    y = a * jax.nn.sigmoid(g)
    y = lax.conv_general_dilated(
        y, conv_dw[:, None, :], window_strides=(1,), padding=[(CONV_KERNEL - 1, 0)],
        dimension_numbers=('NWC', 'WIO', 'NWC'), feature_group_count=CONV_WIDTH)
    y = y + conv_b
    y = jax.nn.silu(layernorm(y, ln_g, ln_b))
    return jnp.einsum('bsc,cd->bsd', y, conv_pw)


def multiscale_pool(u, pool_w, pool_scale):
    B, S, _ = u.shape
    uf = u.astype(jnp.float32)
    c = jnp.cumsum(uf, axis=1)
    c_pad = jnp.concatenate([jnp.zeros((B, POOL_PAD, POOL_WIDTH), jnp.float32), c], axis=1)
    t = jnp.arange(S)
    outs = []
    for gi, w in enumerate(POOL_WINDOWS):
        sl = slice(gi * POOL_GROUP_DIM, (gi + 1) * POOL_GROUP_DIM)
        win_sum = c[:, :, sl] - c_pad[:, POOL_PAD - w:POOL_PAD - w + S, sl]
        count = jnp.minimum(t + 1, w).astype(jnp.float32)[None, :, None]
        outs.append(win_sum / count - uf[:, :, sl])
    y = jnp.stack(outs, axis=2).astype(u.dtype)
    y = jnp.einsum('bsgc,gcd->bsgd', y, pool_w).reshape(B, S, POOL_WIDTH)
    return y * pool_scale


def setup_inputs(seed: int = 0) -> dict:
    key = jax.random.key(seed)
    ks = jax.random.split(key, 24)
    nrm = jax.random.normal
    f32 = jnp.float32
    L = DEPTH
    x = nrm(ks[0], (BATCH, SEQ, D_MODEL), f32)
    p = nrm(ks[1], (DEPTH, BATCH, SEQ, PLE_DIM), f32)
    offs = jax.random.randint(ks[2], (BATCH, 1), 0, MAX_POS_OFFSET, dtype=jnp.int32)
    positions = (offs + jnp.arange(SEQ, dtype=jnp.int32)[None, :]).astype(jnp.int32)
    return {
        "x": x,
        "p": p,
        "positions": positions,
        "norm_mix_pre": 1.0 + 0.1 * nrm(ks[3], (L, D_MODEL), f32),
        "w_in": nrm(ks[4], (L, D_MODEL, IN_WIDTH), f32) * D_MODEL ** -0.5,
        "conv_dw": nrm(ks[5], (L, CONV_KERNEL, CONV_WIDTH), f32) * CONV_KERNEL ** -0.5,
        "conv_b": 0.02 * nrm(ks[6], (L, CONV_WIDTH), f32),
        "conv_ln_g": 1.0 + 0.1 * nrm(ks[7], (L, CONV_WIDTH), f32),
        "conv_ln_b": 0.02 * nrm(ks[8], (L, CONV_WIDTH), f32),
        "conv_pw": nrm(ks[9], (L, CONV_WIDTH, CONV_WIDTH), f32) * CONV_WIDTH ** -0.5,
        "pool_w": nrm(ks[10], (L, POOL_GROUPS, POOL_GROUP_DIM, POOL_GROUP_DIM), f32) * POOL_GROUP_DIM ** -0.5,
        "pool_scale": 1.0 + 0.1 * nrm(ks[11], (L, POOL_WIDTH), f32),
        "w_out": nrm(ks[12], (L, MIX_WIDTH, D_MODEL), f32) * MIX_WIDTH ** -0.5,
        "norm_mix_post": 1.0 + 0.1 * nrm(ks[13], (L, D_MODEL), f32),
        "norm_mlp_pre": 1.0 + 0.1 * nrm(ks[14], (L, D_MODEL), f32),
        "w_up": nrm(ks[15], (L, D_MODEL, D_FF), f32) * D_MODEL ** -0.5,
        "w_down": nrm(ks[16], (L, D_FF, D_MODEL), f32) * D_FF ** -0.5,
        "norm_mlp_post": 1.0 + 0.1 * nrm(ks[17], (L, D_MODEL), f32),
        "ple_proj": nrm(ks[18], (L, PLE_DIM, D_MODEL), f32) * PLE_DIM ** -0.5,
        "ple_gate": nrm(ks[19], (L, D_MODEL, D_MODEL), f32) * D_MODEL ** -0.5,
    }


def reference(x, p, positions, norm_mix_pre, w_in, conv_dw, conv_b, conv_ln_g, conv_ln_b,
              conv_pw, pool_w, pool_scale, w_out, norm_mix_post, norm_mlp_pre, w_up, w_down,
              norm_mlp_post, ple_proj, ple_gate):
    B, S, _ = x.shape
    cos, sin = rope_tables(positions)
    h = x
    for i in range(DEPTH):
        a = rmsnorm(h, norm_mix_pre[i])
        u = jnp.einsum('bsd,de->bse', a, w_in[i])
        q, k, v, qi, ki, wi, u_conv, u_pool = jnp.split(u, IN_SPLITS, axis=-1)
        q = partial_rope(q.reshape(B, S, ATTN_HEADS, HEAD_DIM), cos, sin)
        k = partial_rope(k.reshape(B, S, KV_HEADS, HEAD_DIM), cos, sin)
        v = v.reshape(B, S, KV_HEADS, HEAD_DIM)
        qi = partial_rope(qi.reshape(B, S, IDX_HEADS, IDX_DIM), cos, sin)
        ki = partial_rope(ki[:, :, None, :], cos, sin)[:, :, 0, :]
        y_attn = dsa_sparse_attention(q, k, v, qi, ki, wi)
        y_conv = conformer_conv(u_conv, conv_dw[i], conv_b[i], conv_ln_g[i],
                                conv_ln_b[i], conv_pw[i])
        y_pool = multiscale_pool(u_pool, pool_w[i], pool_scale[i])
        mix = jnp.concatenate([y_attn, y_conv, y_pool], axis=-1)
        mix = jnp.einsum('bse,ed->bsd', mix, w_out[i])
        h = h + rmsnorm(mix, norm_mix_post[i])
        m = rmsnorm(h, norm_mlp_pre[i])
        m = jnp.square(jax.nn.relu(jnp.einsum('bsd,df->bsf', m, w_up[i])))
        m = jnp.einsum('bsf,fd->bsd', m, w_down[i])
        h = h + rmsnorm(m, norm_mlp_post[i])
        gate = jax.nn.sigmoid(jnp.einsum('bsd,de->bse', h, ple_gate[i]))
        h = h + gate * jnp.einsum('bsp,pd->bsd', p[i], ple_proj[i])
    return h
```

```python
import functools

import numpy as np
import jax
import jax.numpy as jnp
from jax import lax
from jax.experimental import pallas as pl
from jax.experimental.pallas import tpu as pltpu

F32 = jnp.float32
BF16 = jnp.bfloat16
I32 = jnp.int32

D_MODEL = 1024
HEAD_DIM = 64
ATTN_WIDTH = 512
ATTN_HEADS = 8
KV_HEADS = 2
GROUP = ATTN_HEADS // KV_HEADS
IDX_HEADS = 16
IDX_DIM = 64
INDEX_TOPK = 256
CONV_WIDTH = 256
CONV_KERNEL = 31
POOL_WIDTH = 256
POOL_GROUPS = 4
POOL_GROUP_DIM = 64
POOL_WINDOWS = (2, 4, 8, 16)
D_FF = 4096
PLE_DIM = 256
ROPE_THETA = 500000.0
ROPE_DIM = 16
ROPE_HALF = ROPE_DIM // 2
NORM_EPS = 1e-6
LN_EPS = 1e-5
ATT_SCALE = HEAD_DIM ** -0.5
IDX_SCALE = (IDX_DIM ** -0.5) * (IDX_HEADS ** -0.5)

LANES = 128
VMEM_LIMIT_BYTES = 56 * 1024 * 1024

COL_Q = 0
COL_K = COL_Q + ATTN_WIDTH
COL_V = COL_K + KV_HEADS * HEAD_DIM
COL_QI = COL_V + KV_HEADS * HEAD_DIM
COL_KIWI = COL_QI + IDX_HEADS * IDX_DIM
COL_CONV = COL_KIWI + LANES
COL_POOL = COL_CONV + 2 * CONV_WIDTH
IN_PAD_WIDTH = COL_POOL + POOL_WIDTH
IN_UNPADDED_SPLIT = ATTN_WIDTH + 2 * KV_HEADS * HEAD_DIM + IDX_HEADS * IDX_DIM + IDX_DIM + IDX_HEADS
WI_LANE0 = IDX_DIM

TM = 512
TQ = 256
KT = 256
QH = 128
HALO = 32
SEARCH_PATTERN = (False, False, True)
COUNT_CHAINS = 8
SEARCH_ROUNDS = 16

NEG_MASK = -1e30
M_INIT = -2e30
INT_MIN = -2 ** 31


def _const_spec(shape):
    zeros = (0,) * len(shape)
    return pl.BlockSpec(shape, lambda *_: zeros, pipeline_mode=pl.Buffered(1))


def _rms(x, gain):
    return x * lax.rsqrt(jnp.mean(x * x, axis=-1, keepdims=True) + NORM_EPS) * gain


def _inproj_kernel(h_ref, pos_ref, g_ref, w_ref, lc_ref,
                   q_ref, k_ref, v_ref, qi_ref, ki_ref, wi_ref, uc_ref, up_ref):
    a = _rms(h_ref[...], g_ref[...]).astype(BF16)
    pos = pos_ref[...].astype(F32)
    ang = pos * lc_ref[0:1, :]
    cos = jnp.cos(ang)
    sin = jnp.sin(ang)
    s_lo = sin * lc_ref[1:2, :]
    s_hi = sin * lc_ref[2:3, :]

    def rope(x):
        return x * cos + pltpu.roll(x, LANES - ROPE_HALF, 1) * s_lo + pltpu.roll(x, ROPE_HALF, 1) * s_hi

    def proj(c0, c1):
        return jnp.dot(a, w_ref[:, c0:c1], preferred_element_type=F32)

    u = proj(COL_Q, COL_K)
    for s in range(ATTN_WIDTH // LANES):
        sl = slice(s * LANES, (s + 1) * LANES)
        q_ref[:, sl] = (rope(u[:, sl]) * ATT_SCALE).astype(BF16)
    k_ref[...] = rope(proj(COL_K, COL_V)).astype(BF16)
    v_ref[...] = proj(COL_V, COL_QI).astype(BF16)
    u = proj(COL_QI, COL_KIWI)
    for s in range(IDX_HEADS * IDX_DIM // LANES):
        sl = slice(s * LANES, (s + 1) * LANES)
        qi_ref[:, sl] = rope(u[:, sl]).astype(BF16)
    u = proj(COL_KIWI, COL_CONV)
    ki_ref[...] = rope(u).astype(BF16)
    wi_ref[...] = u * IDX_SCALE
    uc_ref[...] = proj(COL_CONV, COL_POOL)
    up_ref[...] = proj(COL_POOL, IN_PAD_WIDTH)


def _inproj(h, pos, gain, w, lane_consts):
    T = h.shape[0]
    row = lambda i: (i, 0)
    out_shape = (
        jax.ShapeDtypeStruct((T, ATTN_WIDTH), BF16),
        jax.ShapeDtypeStruct((T, LANES), BF16),
        jax.ShapeDtypeStruct((T, LANES), BF16),
        jax.ShapeDtypeStruct((T, IDX_HEADS * IDX_DIM), BF16),
        jax.ShapeDtypeStruct((T, LANES), BF16),
        jax.ShapeDtypeStruct((T, LANES), F32),
        jax.ShapeDtypeStruct((T, 2 * CONV_WIDTH), F32),
        jax.ShapeDtypeStruct((T, POOL_WIDTH), F32),
    )
    return pl.pallas_call(
        _inproj_kernel,
        out_shape=out_shape,
        grid=(T // TM,),
        in_specs=[
            pl.BlockSpec((TM, D_MODEL), row),
            pl.BlockSpec((TM, 1), row),
            _const_spec((1, D_MODEL)),
            _const_spec((D_MODEL, IN_PAD_WIDTH)),
            _const_spec((8, LANES)),
        ],
        out_specs=tuple(pl.BlockSpec((TM, s.shape[1]), row) for s in out_shape),
        compiler_params=pltpu.CompilerParams(
            dimension_semantics=("parallel",), vmem_limit_bytes=VMEM_LIMIT_BYTES),
        name="inproj",
    )(h, pos, gain, w, lane_consts)


_NT = (((1,), (1,)), ((), ()))


def _key_rows(kt):
    return pl.ds(pl.multiple_of(kt * KT, KT), KT)


def _sortable(x):
    bits = pltpu.bitcast(x, I32)
    return bits ^ ((bits >> 31) & 0x7FFFFFFF)


def _indexer(nkt, q_start, qi_ref, wi_ref, ki_ref, idx_ref, qis_ref, wb_ref):
    for hh in range(IDX_HEADS):
        qis_ref[hh] = qi_ref[:, hh * IDX_DIM:(hh + 1) * IDX_DIM]
        wb_ref[hh] = jnp.broadcast_to(wi_ref[:, WI_LANE0 + hh:WI_LANE0 + hh + 1], (TQ, LANES))

    def key_tile(kt, carry):
        ki = ki_ref[_key_rows(kt), 0:IDX_DIM]
        for qh in range(TQ // QH):
            qs = slice(qh * QH, (qh + 1) * QH)
            acc = [None] * (KT // LANES)
            for hh in range(IDX_HEADS):
                d = lax.dot_general(qis_ref[hh, qs, :], ki, _NT, preferred_element_type=F32)
                w = wb_ref[hh, qs, :]
                for c in range(KT // LANES):
                    term = jnp.maximum(d[:, c * LANES:(c + 1) * LANES], 0.0) * w
                    acc[c] = term if acc[c] is None else acc[c] + term
            qpos = lax.broadcasted_iota(I32, (QH, LANES), 0) + (q_start + qh * QH)
            for c in range(KT // LANES):
                kpos = lax.broadcasted_iota(I32, (QH, LANES), 1) + (kt * KT + c * LANES)
                blk = jnp.where(kpos <= qpos, acc[c], -jnp.inf)
                rows = pl.ds(pl.multiple_of(kt * KT + c * LANES, LANES), LANES)
                idx_ref[rows, qs] = blk.T
        return carry

    lax.fori_loop(0, nkt, key_tile, 0)


def _over_key_tiles(nkt, idx_ref, fn, init):
    return lax.fori_loop(0, nkt, lambda kt, c: fn(idx_ref[_key_rows(kt), :], c), init)


def _select(nkt, k_sel, idx_ref, lo_ref):
    kf = float(k_sel)

    def stats(x, c):
        cmax, cmin, cfin = c
        fin = x > -jnp.inf
        return (jnp.maximum(cmax, jnp.max(x, axis=0, keepdims=True)),
                jnp.minimum(cmin, jnp.min(jnp.where(fin, x, jnp.inf), axis=0, keepdims=True)),
                cfin + jnp.sum(jnp.where(fin, 1.0, 0.0), axis=0, keepdims=True))

    row = lambda v: jnp.full((1, TQ), v, F32)
    cmax, cmin, cfin = _over_key_tiles(nkt, idx_ref, stats, (row(-jnp.inf), row(jnp.inf), row(0.0)))

    @pl.when(nkt % 2 == 1)
    def _():
        idx_ref[_key_rows(nkt), :] = jnp.full((KT, TQ), -jnp.inf, F32)
    npairs = (nkt + 1) // 2

    def count_ge(t):
        def pair(i, acc):
            x = idx_ref[pl.ds(pl.multiple_of(i * (2 * KT), 2 * KT), 2 * KT), :]
            ones = jnp.where(x >= t, 1.0, 0.0)
            parts = [ones[r * 8:(r + 1) * 8, :] for r in range(2 * KT // 8)]
            lanes = [sum(parts[a::COUNT_CHAINS][1:], parts[a]) for a in range(COUNT_CHAINS)]
            while len(lanes) > 1:
                lanes = [lanes[a] + lanes[a + 1] for a in range(0, len(lanes), 2)]
            return acc + lanes[0]
        acc = lax.fori_loop(0, npairs, pair, jnp.zeros((8, TQ), F32))
        return jnp.sum(acc, axis=0, keepdims=True)

    def step(state, bisect):
        lo, hi, cnt_lo, cnt_hi = state
        frac = 0.5 if bisect else (cnt_lo - kf + 0.5) / (cnt_lo - cnt_hi)
        mid = lo + (hi - lo) * frac
        cnt = count_ge(mid)
        up = cnt >= kf
        return (jnp.where(up, mid, lo), jnp.where(up, hi, mid),
                jnp.where(up, cnt, cnt_lo), jnp.where(up, cnt_hi, cnt))

    def unsettled(cnt_lo):
        return jnp.max(jnp.where(cnt_lo != kf, 1.0, 0.0))

    def cond(c):
        return (c[0] < SEARCH_ROUNDS) & (c[1] > 0.0)

    def body(c):
        state = c[2:]
        for bisect in SEARCH_PATTERN:
            state = step(state, bisect)
        return (c[0] + 1, unsettled(state[2])) + state

    out = lax.while_loop(cond, body, (jnp.int32(0), unsettled(cfin), cmin, cmax, cfin, row(1.0)))
    lo_ref[...] = out[2]
    return out[1]


def _select_exact(nkt, k_sel, idx_ref, lo_ref):
    def count(pred):
        return _over_key_tiles(
            nkt, idx_ref,
            lambda x, c: c + jnp.sum(pred(_sortable(x)).astype(I32), axis=0, keepdims=True),
            jnp.zeros((1, TQ), I32))

    def search(i, prefix):
        cand = prefix | lax.shift_left(jnp.int32(1), 31 - i)
        return jnp.where(count(lambda key: key >= (cand ^ INT_MIN)) >= k_sel, cand, prefix)

    thr = lax.fori_loop(0, 32, search, jnp.zeros((1, TQ), I32)) ^ INT_MIN
    need = (k_sel - count(lambda key: key > thr)).astype(F32)
    r = lax.broadcasted_iota(I32, (KT, KT), 0)
    c = lax.broadcasted_iota(I32, (KT, KT), 1)
    earlier = (c < r).astype(BF16)

    def mark(kt, carry):
        x = idx_ref[_key_rows(kt), :]
        key = _sortable(x)
        eq = key == thr
        eqf = jnp.where(eq, 1.0, 0.0)
        rank = jnp.dot(earlier, eqf.astype(BF16), preferred_element_type=F32) + carry
        keep = ((key > thr) | (eq & (rank < need))) & (x > -jnp.inf)
        idx_ref[_key_rows(kt), :] = jnp.where(keep, 1.0, -jnp.inf)
        return carry + jnp.sum(eqf, axis=0, keepdims=True)

    lax.fori_loop(0, nkt, mark, jnp.zeros((1, TQ), F32))
    lo_ref[...] = jnp.full((1, TQ), 0.5, F32)


def _attend(nkt, q_ref, k_ref, vt_ref, idx_ref, lo_ref, qs_ref, m_ref, l_ref, o_ref, y_ref):
    for hh in range(ATTN_HEADS):
        g, jj = divmod(hh, GROUP)
        qs_ref[g, jj * TQ:(jj + 1) * TQ, :] = q_ref[:, hh * HEAD_DIM:(hh + 1) * HEAD_DIM]
    m_ref[...] = jnp.full(m_ref.shape, M_INIT, F32)
    l_ref[...] = jnp.zeros(l_ref.shape, F32)
    o_ref[...] = jnp.zeros(o_ref.shape, F32)
    lo = lo_ref[...]

    def key_tile(kt, carry):
        rows = _key_rows(kt)
        bias = jnp.where(idx_ref[rows, :] >= lo, 0.0, NEG_MASK)
        scores = [lax.dot_general(k_ref[rows, g * HEAD_DIM:(g + 1) * HEAD_DIM], qs_ref[g], _NT,
                                  preferred_element_type=F32) for g in range(KV_HEADS)]
        for g in range(KV_HEADS):
            probs, alphas = [], []
            for jj in range(GROUP):
                hh = g * GROUP + jj
                s = scores[g][:, jj * TQ:(jj + 1) * TQ] + bias
                m_old = m_ref[hh]
                m_new = jnp.maximum(m_old, jnp.max(s, axis=0, keepdims=True))
                alpha = jnp.exp(m_old - m_new)
                p = jnp.exp(s - m_new)
                l_ref[hh] = l_ref[hh] * alpha + jnp.sum(p, axis=0, keepdims=True)
                m_ref[hh] = m_new
                probs.append(p.astype(BF16))
                alphas.append(alpha)
            pv = jnp.dot(vt_ref[kt, g * HEAD_DIM:(g + 1) * HEAD_DIM, :], jnp.concatenate(probs, axis=1),
                         preferred_element_type=F32)
            for jj in range(GROUP):
                hh = g * GROUP + jj
                orow = slice(hh * HEAD_DIM, (hh + 1) * HEAD_DIM)
                o_ref[orow, :] = o_ref[orow, :] * alphas[jj] + pv[:, jj * TQ:(jj + 1) * TQ]
        return carry

    lax.fori_loop(0, nkt, key_tile, 0)
    for hh in range(ATTN_HEADS):
        orow = slice(hh * HEAD_DIM, (hh + 1) * HEAD_DIM)
        o_ref[orow, :] = o_ref[orow, :] / l_ref[hh]
    y_ref[...] = o_ref[...].T.astype(BF16)


def _attn_kernel(seq, k_sel, qi_ref, wi_ref, q_ref, ki_ref, k_ref, v_ref, y_ref,
                 idx_ref, lo_ref, qis_ref, wb_ref, qs_ref, vt_ref, m_ref, l_ref, o_ref):
    j = pl.program_id(1)
    nkt = j + 1
    q_start = j * TQ

    @pl.when(j == 0)
    def _():
        for kt in range(seq // KT):
            vt_ref[kt] = v_ref[kt * KT:(kt + 1) * KT, :].astype(F32).T.astype(BF16)

    all_selected = (j + 1) * TQ <= k_sel

    @pl.when(all_selected)
    def _():
        def fill(kt, carry):
            kpos = lax.broadcasted_iota(I32, (KT, TQ), 0) + kt * KT
            qpos = lax.broadcasted_iota(I32, (KT, TQ), 1) + q_start
            idx_ref[_key_rows(kt), :] = jnp.where(kpos <= qpos, 1.0, -jnp.inf)
            return carry
        lax.fori_loop(0, nkt, fill, 0)
        lo_ref[...] = jnp.full((1, TQ), 0.5, F32)

    @pl.when(jnp.logical_not(all_selected))
    def _():
        _indexer(nkt, q_start, qi_ref, wi_ref, ki_ref, idx_ref, qis_ref, wb_ref)
        failed = _select(nkt, k_sel, idx_ref, lo_ref)

        @pl.when(failed > 0.0)
        def _():
            _select_exact(nkt, k_sel, idx_ref, lo_ref)

    _attend(nkt, q_ref, k_ref, vt_ref, idx_ref, lo_ref, qs_ref, m_ref, l_ref, o_ref, y_ref)


def _attention(qi, wi, q, ki, k, v, batch, seq):
    T = batch * seq
    nq = seq // TQ
    k_sel = min(INDEX_TOPK, seq // 4)
    tile = lambda b, j: (b * nq + j, 0)
    per_batch = lambda b, j: (b, 0)
    return pl.pallas_call(
        functools.partial(_attn_kernel, seq, k_sel),
        out_shape=jax.ShapeDtypeStruct((T, ATTN_WIDTH), BF16),
        grid=(batch, nq),
        in_specs=[
            pl.BlockSpec((TQ, IDX_HEADS * IDX_DIM), tile),
            pl.BlockSpec((TQ, LANES), tile),
            pl.BlockSpec((TQ, ATTN_WIDTH), tile),
            pl.BlockSpec((seq, LANES), per_batch),
            pl.BlockSpec((seq, LANES), per_batch),
            pl.BlockSpec((seq, LANES), per_batch),
        ],
        out_specs=pl.BlockSpec((TQ, ATTN_WIDTH), tile),
        scratch_shapes=[
            pltpu.VMEM((seq, TQ), F32),
            pltpu.VMEM((1, TQ), F32),
            pltpu.VMEM((IDX_HEADS, TQ, IDX_DIM), BF16),
            pltpu.VMEM((IDX_HEADS, TQ, LANES), F32),
            pltpu.VMEM((KV_HEADS, GROUP * TQ, HEAD_DIM), BF16),
            pltpu.VMEM((seq // KT, LANES, KT), BF16),
            pltpu.VMEM((ATTN_HEADS, 1, TQ), F32),
            pltpu.VMEM((ATTN_HEADS, 1, TQ), F32),
            pltpu.VMEM((ATTN_WIDTH, TQ), F32),
        ],
        compiler_params=pltpu.CompilerParams(
            dimension_semantics=("arbitrary", "arbitrary"), vmem_limit_bytes=VMEM_LIMIT_BYTES),
        name="dsa_attention",
    )(qi, wi, q, ki, k, v)


def _mix_kernel(y_ref, uc_ref, uch_ref, up_ref, uph_ref, h_ref, dw_ref, cb_ref, lg_ref, lb_ref,
                pw_ref, plw_ref, psc_ref, wo_ref, gpost_ref, o_ref, ypad_ref, upad_ref):
    j = pl.program_id(1)
    has_prev = (j > 0).astype(F32)

    def glu(u):
        return u[:, :CONV_WIDTH] * jax.nn.sigmoid(u[:, CONV_WIDTH:])

    ypad_ref[0:HALO, :] = glu(uch_ref[...]) * has_prev
    ypad_ref[HALO:HALO + TM, :] = glu(uc_ref[...])
    acc = None
    for tap in range(CONV_KERNEL):
        off = HALO - (CONV_KERNEL - 1) + tap
        term = ypad_ref[off:off + TM, :] * dw_ref[tap:tap + 1, :]
        acc = term if acc is None else acc + term
    yc = acc + cb_ref[...]
    mu = jnp.mean(yc, axis=-1, keepdims=True)
    var = jnp.mean(jnp.square(yc - mu), axis=-1, keepdims=True)
    yn = (yc - mu) * lax.rsqrt(var + LN_EPS) * lg_ref[...] + lb_ref[...]
    ys = yn * jax.nn.sigmoid(yn)
    y_conv = jnp.dot(ys.astype(BF16), pw_ref[...], preferred_element_type=F32)

    upad_ref[0:HALO, :] = uph_ref[...] * has_prev
    u0 = up_ref[...]
    upad_ref[HALO:HALO + TM, :] = u0

    def back(i):
        return upad_ref[HALO - i:HALO - i + TM, :]

    sums = []
    run = u0
    nxt = 1
    for w in POOL_WINDOWS:
        while nxt < w:
            run = run + back(nxt)
            nxt += 1
        sums.append(run)
    lane = lax.broadcasted_iota(I32, (TM, POOL_WIDTH), 1)
    win_sum = sums[-1]
    win = jnp.full((TM, POOL_WIDTH), POOL_WINDOWS[-1], I32)
    for gi in range(POOL_GROUPS - 2, -1, -1):
        in_group = lane < (gi + 1) * POOL_GROUP_DIM
        win_sum = jnp.where(in_group, sums[gi], win_sum)
        win = jnp.where(in_group, POOL_WINDOWS[gi], win)
    t = lax.broadcasted_iota(I32, (TM, POOL_WIDTH), 0) + j * TM
    count = jnp.minimum(t + 1, win).astype(F32)
    pooled = win_sum / count - u0
    y_pool = jnp.dot(pooled.astype(BF16), plw_ref[...], preferred_element_type=F32) * psc_ref[...]

    c0, c1 = ATTN_WIDTH, ATTN_WIDTH + CONV_WIDTH
    mix = (jnp.dot(y_ref[...], wo_ref[0:c0, :], preferred_element_type=F32)
           + jnp.dot(y_conv.astype(BF16), wo_ref[c0:c1, :], preferred_element_type=F32)
           + jnp.dot(y_pool.astype(BF16), wo_ref[c1:, :], preferred_element_type=F32))
    o_ref[...] = h_ref[...] + _rms(mix, gpost_ref[...])


def _mix(y, uc, up, h, dw, cb, lg, lb, pw, plw, psc, wo, gpost, batch, seq):
    T = batch * seq
    nt = seq // TM
    tile = lambda b, j: (b * nt + j, 0)
    halo_per_tile = TM // HALO
    halo = lambda b, j: (b * (seq // HALO) + jnp.maximum(j * halo_per_tile - 1, 0), 0)
    return pl.pallas_call(
        _mix_kernel,
        out_shape=jax.ShapeDtypeStruct((T, D_MODEL), F32),
        grid=(batch, nt),
        in_specs=[
            pl.BlockSpec((TM, ATTN_WIDTH), tile),
            pl.BlockSpec((TM, 2 * CONV_WIDTH), tile),
            pl.BlockSpec((HALO, 2 * CONV_WIDTH), halo),
            pl.BlockSpec((TM, POOL_WIDTH), tile),
            pl.BlockSpec((HALO, POOL_WIDTH), halo),
            pl.BlockSpec((TM, D_MODEL), tile),
            _const_spec((HALO, CONV_WIDTH)),
            _const_spec((1, CONV_WIDTH)),
            _const_spec((1, CONV_WIDTH)),
            _const_spec((1, CONV_WIDTH)),
            _const_spec((CONV_WIDTH, CONV_WIDTH)),
            _const_spec((POOL_WIDTH, POOL_WIDTH)),
            _const_spec((1, POOL_WIDTH)),
            _const_spec((D_MODEL, D_MODEL)),
            _const_spec((1, D_MODEL)),
        ],
        out_specs=pl.BlockSpec((TM, D_MODEL), tile),
        scratch_shapes=[
            pltpu.VMEM((HALO + TM, CONV_WIDTH), F32),
            pltpu.VMEM((HALO + TM, POOL_WIDTH), F32),
        ],
        compiler_params=pltpu.CompilerParams(
            dimension_semantics=("parallel", "parallel"), vmem_limit_bytes=VMEM_LIMIT_BYTES),
        name="mix_out",
    )(y, uc, uc, up, up, h, dw, cb, lg, lb, pw, plw, psc, wo, gpost)


FF_CHUNK = 1024


def _mlp_kernel(h_ref, p_ref, gpre_ref, wup_ref, wdn_ref, gpost_ref, wg_ref, wp_ref, o_ref):
    h = h_ref[...]
    m = _rms(h, gpre_ref[...]).astype(BF16)
    acc = None
    for c in range(D_FF // FF_CHUNK):
        cs = slice(c * FF_CHUNK, (c + 1) * FF_CHUNK)
        r = jnp.maximum(jnp.dot(m, wup_ref[:, cs], preferred_element_type=F32), 0.0)
        d = jnp.dot((r * r).astype(BF16), wdn_ref[cs, :], preferred_element_type=F32)
        acc = d if acc is None else acc + d
    h2 = h + _rms(acc, gpost_ref[...])
    gate = jax.nn.sigmoid(jnp.dot(h2.astype(BF16), wg_ref[...], preferred_element_type=F32))
    emb = jnp.dot(p_ref[...].astype(BF16), wp_ref[...], preferred_element_type=F32)
    o_ref[...] = h2 + gate * emb


def _mlp(h, p, gpre, wup, wdn, gpost, wg, wp):
    T = h.shape[0]
    row = lambda i: (i, 0)
    return pl.pallas_call(
        _mlp_kernel,
        out_shape=jax.ShapeDtypeStruct((T, D_MODEL), F32),
        grid=(T // TM,),
        in_specs=[
            pl.BlockSpec((TM, D_MODEL), row),
            pl.BlockSpec((TM, PLE_DIM), row),
            _const_spec((1, D_MODEL)),
            _const_spec((D_MODEL, D_FF)),
            _const_spec((D_FF, D_MODEL)),
            _const_spec((1, D_MODEL)),
            _const_spec((D_MODEL, D_MODEL)),
            _const_spec((PLE_DIM, D_MODEL)),
        ],
        out_specs=pl.BlockSpec((TM, D_MODEL), row),
        compiler_params=pltpu.CompilerParams(
            dimension_semantics=("parallel",), vmem_limit_bytes=VMEM_LIMIT_BYTES),
        name="mlp_ple",
    )(h, p, gpre, wup, wdn, gpost, wg, wp)


def _lane_constants():
    inv_freq = ROPE_THETA ** (-jnp.arange(0, ROPE_DIM, 2, dtype=F32) / ROPE_DIM)
    lane = np.arange(LANES) % HEAD_DIM
    rot = lane < ROPE_DIM
    freq = jnp.where(jnp.asarray(rot), inv_freq[lane % ROPE_HALF], 0.0)
    lo = jnp.asarray(np.where(lane < ROPE_HALF, -1.0, 0.0), F32)
    hi = jnp.asarray(np.where(rot & (lane >= ROPE_HALF), 1.0, 0.0), F32)
    return jnp.concatenate([jnp.stack([freq, lo, hi]), jnp.zeros((5, LANES), F32)], axis=0)


def kernel(x, p, positions, norm_mix_pre, w_in, conv_dw, conv_b, conv_ln_g, conv_ln_b, conv_pw,
           pool_w, pool_scale, w_out, norm_mix_post, norm_mlp_pre, w_up, w_down, norm_mlp_post,
           ple_proj, ple_gate):
    batch, seq, d_model = x.shape
    depth = w_in.shape[0]
    assert d_model == D_MODEL and seq % TQ == 0 and seq % TM == 0 and TQ == KT
    T = batch * seq

    w_in_pad = jnp.concatenate(
        [w_in[:, :, :IN_UNPADDED_SPLIT],
         jnp.zeros((depth, D_MODEL, COL_CONV - IN_UNPADDED_SPLIT), w_in.dtype),
         w_in[:, :, IN_UNPADDED_SPLIT:]], axis=2).astype(BF16)
    dw_pad = jnp.concatenate([conv_dw, jnp.zeros((depth, HALO - CONV_KERNEL, CONV_WIDTH), F32)], axis=1)
    pool_bd = jnp.einsum('lgcd,gh->lgchd', pool_w, jnp.eye(POOL_GROUPS, dtype=F32)).reshape(
        depth, POOL_WIDTH, POOL_WIDTH).astype(BF16)
    conv_pw_b = conv_pw.astype(BF16)
    w_out_b = w_out.astype(BF16)
    w_up_b = w_up.astype(BF16)
    w_down_b = w_down.astype(BF16)
    gate_b = ple_gate.astype(BF16)
    proj_b = ple_proj.astype(BF16)
    vec = lambda a, i: a[i].reshape(1, -1)

    lane_consts = _lane_constants()
    pos = positions.reshape(T, 1)
    h = x.reshape(T, D_MODEL)
    p2 = p.reshape(depth, T, PLE_DIM)
    for i in range(depth):
        q, k, v, qi, ki, wi, uc, up = _inproj(h, pos, vec(norm_mix_pre, i), w_in_pad[i], lane_consts)
        y = _attention(qi, wi, q, ki, k, v, batch, seq)
        h = _mix(y, uc, up, h, dw_pad[i], vec(conv_b, i), vec(conv_ln_g, i), vec(conv_ln_b, i),
                 conv_pw_b[i], pool_bd[i], vec(pool_scale, i), w_out_b[i], vec(norm_mix_post, i),
                 batch, seq)
        h = _mlp(h, p2[i], vec(norm_mlp_pre, i), w_up_b[i], w_down_b[i], vec(norm_mlp_post, i),
                 gate_b[i], proj_b[i])
    return h.reshape(batch, seq, D_MODEL)
```

```python
import functools

import numpy as np
import jax
import jax.numpy as jnp
from jax import lax
from jax.experimental import pallas as pl
from jax.experimental.pallas import tpu as pltpu

F32 = jnp.float32
BF16 = jnp.bfloat16
I32 = jnp.int32

D_MODEL = 1024
HEAD_DIM = 64
ATTN_WIDTH = 512
ATTN_HEADS = 8
KV_HEADS = 2
GROUP = ATTN_HEADS // KV_HEADS
IDX_HEADS = 16
IDX_DIM = 64
INDEX_TOPK = 256
CONV_WIDTH = 256
CONV_KERNEL = 31
POOL_WIDTH = 256
POOL_GROUPS = 4
POOL_GROUP_DIM = 64
POOL_WINDOWS = (2, 4, 8, 16)
D_FF = 4096
PLE_DIM = 256
ROPE_THETA = 500000.0
ROPE_DIM = 16
ROPE_HALF = ROPE_DIM // 2
NORM_EPS = 1e-6
LN_EPS = 1e-5
ATT_SCALE = HEAD_DIM ** -0.5
Q_SCALE = ATT_SCALE * float(np.log2(np.e))
IDX_SCALE = (IDX_DIM ** -0.5) * (IDX_HEADS ** -0.5)

LANES = 128
VMEM_LIMIT_BYTES = 56 * 1024 * 1024

COL_Q = 0
COL_K = COL_Q + ATTN_WIDTH
COL_V = COL_K + KV_HEADS * HEAD_DIM
COL_QI = COL_V + KV_HEADS * HEAD_DIM
COL_KIWI = COL_QI + IDX_HEADS * IDX_DIM
COL_CONV = COL_KIWI + LANES
COL_POOL = COL_CONV + 2 * CONV_WIDTH
IN_PAD_WIDTH = COL_POOL + POOL_WIDTH
IN_UNPADDED_SPLIT = ATTN_WIDTH + 2 * KV_HEADS * HEAD_DIM + IDX_HEADS * IDX_DIM + IDX_DIM + IDX_HEADS
WI_LANE0 = IDX_DIM

TM = 512
TQ = 256
KT = 256
QH = 128
HALO = 32
SEARCH_UNROLL = 4
COUNT_CHAINS = 8
SEARCH_ROUNDS = 12
VT_ROWS = HEAD_DIM + 16

NEG_MASK = -1e30
M_INIT = -2e30
INT_MIN = -2 ** 31


def _const_spec(shape):
    zeros = (0,) * len(shape)
    return pl.BlockSpec(shape, lambda *_: zeros, pipeline_mode=pl.Buffered(1))


def _rms(x, gain):
    return x * lax.rsqrt(jnp.mean(x * x, axis=-1, keepdims=True) + NORM_EPS) * gain


def _inproj_kernel(h_ref, pos_ref, g_ref, w_ref, lc_ref,
                   q_ref, k_ref, v_ref, qi_ref, ki_ref, wi_ref, uc_ref, up_ref):
    a = _rms(h_ref[...], g_ref[...]).astype(BF16)
    pos = pos_ref[...].astype(F32)
    ang = pos * lc_ref[0:1, :]
    cos = jnp.cos(ang)
    sin = jnp.sin(ang)
    s_lo = sin * lc_ref[1:2, :]
    s_hi = sin * lc_ref[2:3, :]

    def rope(x):
        return x * cos + pltpu.roll(x, LANES - ROPE_HALF, 1) * s_lo + pltpu.roll(x, ROPE_HALF, 1) * s_hi

    def proj(c0, c1):
        return jnp.dot(a, w_ref[:, c0:c1], preferred_element_type=F32)

    u = proj(COL_Q, COL_K)
    for s in range(ATTN_WIDTH // LANES):
        sl = slice(s * LANES, (s + 1) * LANES)
        q_ref[:, sl] = (rope(u[:, sl]) * Q_SCALE).astype(BF16)
    k_ref[...] = rope(proj(COL_K, COL_V)).astype(BF16)
    v_ref[...] = proj(COL_V, COL_QI).astype(BF16)
    u = proj(COL_QI, COL_KIWI)
    for s in range(IDX_HEADS * IDX_DIM // LANES):
        sl = slice(s * LANES, (s + 1) * LANES)
        qi_ref[:, sl] = rope(u[:, sl]).astype(BF16)
    u = proj(COL_KIWI, COL_CONV)
    ki_ref[...] = rope(u).astype(BF16)
    wi_ref[...] = u * IDX_SCALE
    uc_ref[...] = proj(COL_CONV, COL_POOL)
    up_ref[...] = proj(COL_POOL, IN_PAD_WIDTH)


def _inproj(h, pos, gain, w, lane_consts):
    T = h.shape[0]
    row = lambda i: (i, 0)
    out_shape = (
        jax.ShapeDtypeStruct((T, ATTN_WIDTH), BF16),
        jax.ShapeDtypeStruct((T, LANES), BF16),
        jax.ShapeDtypeStruct((T, LANES), BF16),
        jax.ShapeDtypeStruct((T, IDX_HEADS * IDX_DIM), BF16),
        jax.ShapeDtypeStruct((T, LANES), BF16),
        jax.ShapeDtypeStruct((T, LANES), F32),
        jax.ShapeDtypeStruct((T, 2 * CONV_WIDTH), F32),
        jax.ShapeDtypeStruct((T, POOL_WIDTH), F32),
    )
    return pl.pallas_call(
        _inproj_kernel,
        out_shape=out_shape,
        grid=(T // TM,),
        in_specs=[
            pl.BlockSpec((TM, D_MODEL), row),
            pl.BlockSpec((TM, 1), row),
            _const_spec((1, D_MODEL)),
            _const_spec((D_MODEL, IN_PAD_WIDTH)),
            _const_spec((8, LANES)),
        ],
        out_specs=tuple(pl.BlockSpec((TM, s.shape[1]), row) for s in out_shape),
        compiler_params=pltpu.CompilerParams(
            dimension_semantics=("parallel",), vmem_limit_bytes=VMEM_LIMIT_BYTES),
        name="inproj",
    )(h, pos, gain, w, lane_consts)


_NT = (((1,), (1,)), ((), ()))


def _key_rows(kt):
    return pl.ds(pl.multiple_of(kt * KT, KT), KT)


def _sortable(x):
    bits = pltpu.bitcast(x, I32)
    return bits ^ ((bits >> 31) & 0x7FFFFFFF)


def _indexer(nkt, q_start, qi_ref, wi_ref, ki_ref, idx_ref, qis_ref, wb_ref):
    for hh in range(IDX_HEADS):
        qis_ref[hh] = qi_ref[:, hh * IDX_DIM:(hh + 1) * IDX_DIM]
        wb_ref[hh] = jnp.broadcast_to(wi_ref[:, WI_LANE0 + hh:WI_LANE0 + hh + 1], (TQ, LANES))

    def key_tile(kt, carry):
        ki = ki_ref[_key_rows(kt), 0:IDX_DIM]
        for qh in range(TQ // QH):
            qs = slice(qh * QH, (qh + 1) * QH)
            acc = [None] * (KT // LANES)
            for hh in range(IDX_HEADS):
                d = lax.dot_general(qis_ref[hh, qs, :], ki, _NT, preferred_element_type=F32)
                w = wb_ref[hh, qs, :]
                for c in range(KT // LANES):
                    term = jnp.maximum(d[:, c * LANES:(c + 1) * LANES], 0.0) * w
                    acc[c] = term if acc[c] is None else acc[c] + term
            qpos = lax.broadcasted_iota(I32, (QH, LANES), 0) + (q_start + qh * QH)
            for c in range(KT // LANES):
                kpos = lax.broadcasted_iota(I32, (QH, LANES), 1) + (kt * KT + c * LANES)
                blk = jnp.where(kpos <= qpos, acc[c], -jnp.inf)
                rows = pl.ds(pl.multiple_of(kt * KT + c * LANES, LANES), LANES)
                idx_ref[rows, qs] = blk.T
        return carry

    lax.fori_loop(0, nkt, key_tile, 0)


def _over_key_tiles(nkt, idx_ref, fn, init):
    return lax.fori_loop(0, nkt, lambda kt, c: fn(idx_ref[_key_rows(kt), :], c), init)


def _select(nkt, k_sel, idx_ref, lo_ref):
    kf = float(k_sel)

    def stats(x, c):
        cmax, cmin, cfin = c
        fin = x > -jnp.inf
        return (jnp.maximum(cmax, jnp.max(x, axis=0, keepdims=True)),
                jnp.minimum(cmin, jnp.min(jnp.where(fin, x, jnp.inf), axis=0, keepdims=True)),
                cfin + jnp.sum(jnp.where(fin, 1.0, 0.0), axis=0, keepdims=True))

    row = lambda v: jnp.full((1, TQ), v, F32)
    cmax, cmin, cfin = _over_key_tiles(nkt, idx_ref, stats, (row(-jnp.inf), row(jnp.inf), row(0.0)))

    @pl.when(nkt % 2 == 1)
    def _():
        idx_ref[_key_rows(nkt), :] = jnp.full((KT, TQ), -jnp.inf, F32)
    npairs = (nkt + 1) // 2

    def count_ge(t):
        def pair(i, acc):
            x = idx_ref[pl.ds(pl.multiple_of(i * (2 * KT), 2 * KT), 2 * KT), :]
            ones = jnp.where(x >= t, 1.0, 0.0)
            parts = [ones[r * 8:(r + 1) * 8, :] for r in range(2 * KT // 8)]
            lanes = [sum(parts[a::COUNT_CHAINS][1:], parts[a]) for a in range(COUNT_CHAINS)]
            while len(lanes) > 1:
                lanes = [lanes[a] + lanes[a + 1] for a in range(0, len(lanes), 2)]
            return acc + lanes[0]
        acc = lax.fori_loop(0, npairs, pair, jnp.zeros((8, TQ), F32))
        return jnp.sum(acc, axis=0, keepdims=True)

    def step(state):
        lo, hi, cnt_lo = state
        mid = lo + (hi - lo) * 0.5
        cnt = count_ge(mid)
        up = cnt >= kf
        return jnp.where(up, mid, lo), jnp.where(up, hi, mid), jnp.where(up, cnt, cnt_lo)

    def unsettled(cnt_lo):
        return jnp.max(jnp.where(cnt_lo != kf, 1.0, 0.0))

    def cond(c):
        return (c[0] < SEARCH_ROUNDS) & (c[1] > 0.0)

    def body(c):
        state = c[2:]
        for _ in range(SEARCH_UNROLL):
            state = step(state)
        return (c[0] + 1, unsettled(state[2])) + state

    out = lax.while_loop(cond, body, (jnp.int32(0), unsettled(cfin), cmin, cmax, cfin))
    lo_ref[...] = out[2]
    return out[1]


def _select_exact(nkt, k_sel, idx_ref, lo_ref):
    def count(pred):
        return _over_key_tiles(
            nkt, idx_ref,
            lambda x, c: c + jnp.sum(pred(_sortable(x)).astype(I32), axis=0, keepdims=True),
            jnp.zeros((1, TQ), I32))

    def search(i, prefix):
        cand = prefix | lax.shift_left(jnp.int32(1), 31 - i)
        return jnp.where(count(lambda key: key >= (cand ^ INT_MIN)) >= k_sel, cand, prefix)

    thr = lax.fori_loop(0, 32, search, jnp.zeros((1, TQ), I32)) ^ INT_MIN
    need = (k_sel - count(lambda key: key > thr)).astype(F32)
    r = lax.broadcasted_iota(I32, (KT, KT), 0)
    c = lax.broadcasted_iota(I32, (KT, KT), 1)
    earlier = (c < r).astype(BF16)

    def mark(kt, carry):
        x = idx_ref[_key_rows(kt), :]
        key = _sortable(x)
        eq = key == thr
        eqf = jnp.where(eq, 1.0, 0.0)
        rank = jnp.dot(earlier, eqf.astype(BF16), preferred_element_type=F32) + carry
        keep = ((key > thr) | (eq & (rank < need))) & (x > -jnp.inf)
        idx_ref[_key_rows(kt), :] = jnp.where(keep, 1.0, -jnp.inf)
        return carry + jnp.sum(eqf, axis=0, keepdims=True)

    lax.fori_loop(0, nkt, mark, jnp.zeros((1, TQ), F32))
    lo_ref[...] = jnp.full((1, TQ), 0.5, F32)


def _attend(nkt, q_ref, k_ref, vt_ref, idx_ref, lo_ref, qs_ref, m_ref, l_ref, o_ref, y_ref):
    for hh in range(ATTN_HEADS):
        g, jj = divmod(hh, GROUP)
        qs_ref[g, jj * TQ:(jj + 1) * TQ, :] = q_ref[:, hh * HEAD_DIM:(hh + 1) * HEAD_DIM]
    m_ref[...] = jnp.full(m_ref.shape, M_INIT, F32)
    l_ref[...] = jnp.zeros(l_ref.shape, F32)
    o_ref[...] = jnp.zeros(o_ref.shape, F32)
    lo = lo_ref[...]

    def key_tile(kt, carry):
        rows = _key_rows(kt)
        bias = jnp.where(idx_ref[rows, :] >= lo, 0.0, NEG_MASK)
        scores = [lax.dot_general(k_ref[rows, g * HEAD_DIM:(g + 1) * HEAD_DIM], qs_ref[g], _NT,
                                  preferred_element_type=F32) for g in range(KV_HEADS)]
        for g in range(KV_HEADS):
            probs, alphas = [], []
            for jj in range(GROUP):
                hh = g * GROUP + jj
                s = scores[g][:, jj * TQ:(jj + 1) * TQ] + bias
                m_old = m_ref[hh]
                m_new = jnp.maximum(m_old, jnp.max(s, axis=0, keepdims=True))
                alphas.append(jnp.exp2(m_old - m_new))
                probs.append(jnp.exp2(s - m_new).astype(BF16))
                m_ref[hh] = m_new
            pv = jnp.dot(vt_ref[kt, g * VT_ROWS:(g + 1) * VT_ROWS, :], jnp.concatenate(probs, axis=1),
                         preferred_element_type=F32)
            for jj in range(GROUP):
                hh = g * GROUP + jj
                cols = slice(jj * TQ, (jj + 1) * TQ)
                orow = slice(hh * HEAD_DIM, (hh + 1) * HEAD_DIM)
                o_ref[orow, :] = o_ref[orow, :] * alphas[jj] + pv[0:HEAD_DIM, cols]
                l_ref[hh] = l_ref[hh] * alphas[jj] + pv[HEAD_DIM:HEAD_DIM + 1, cols]
        return carry

    lax.fori_loop(0, nkt, key_tile, 0)
    for hh in range(ATTN_HEADS):
        orow = slice(hh * HEAD_DIM, (hh + 1) * HEAD_DIM)
        o_ref[orow, :] = o_ref[orow, :] / l_ref[hh]
    y_ref[...] = o_ref[...].T.astype(BF16)


def _attn_kernel(seq, k_sel, qi_ref, wi_ref, q_ref, ki_ref, k_ref, v_ref, y_ref,
                 idx_ref, lo_ref, qis_ref, wb_ref, qs_ref, vt_ref, m_ref, l_ref, o_ref):
    j = pl.program_id(1)
    nkt = j + 1
    q_start = j * TQ

    @pl.when(j == 0)
    def _():
        for kt in range(seq // KT):
            v_t = v_ref[kt * KT:(kt + 1) * KT, :].astype(F32).T.astype(BF16)
            for g in range(KV_HEADS):
                vt_ref[kt, g * VT_ROWS:g * VT_ROWS + HEAD_DIM, :] = v_t[g * HEAD_DIM:(g + 1) * HEAD_DIM, :]
                vt_ref[kt, g * VT_ROWS + HEAD_DIM:(g + 1) * VT_ROWS, :] = jnp.ones(
                    (VT_ROWS - HEAD_DIM, KT), BF16)

    all_selected = (j + 1) * TQ <= k_sel

    @pl.when(all_selected)
    def _():
        def fill(kt, carry):
            kpos = lax.broadcasted_iota(I32, (KT, TQ), 0) + kt * KT
            qpos = lax.broadcasted_iota(I32, (KT, TQ), 1) + q_start
            idx_ref[_key_rows(kt), :] = jnp.where(kpos <= qpos, 1.0, -jnp.inf)
            return carry
        lax.fori_loop(0, nkt, fill, 0)
        lo_ref[...] = jnp.full((1, TQ), 0.5, F32)

    @pl.when(jnp.logical_not(all_selected))
    def _():
        _indexer(nkt, q_start, qi_ref, wi_ref, ki_ref, idx_ref, qis_ref, wb_ref)
        failed = _select(nkt, k_sel, idx_ref, lo_ref)

        @pl.when(failed > 0.0)
        def _():
            _select_exact(nkt, k_sel, idx_ref, lo_ref)

    _attend(nkt, q_ref, k_ref, vt_ref, idx_ref, lo_ref, qs_ref, m_ref, l_ref, o_ref, y_ref)


def _attention(qi, wi, q, ki, k, v, batch, seq):
    T = batch * seq
    nq = seq // TQ
    k_sel = min(INDEX_TOPK, seq // 4)
    tile = lambda b, j: (b * nq + j, 0)
    per_batch = lambda b, j: (b, 0)
    return pl.pallas_call(
        functools.partial(_attn_kernel, seq, k_sel),
        out_shape=jax.ShapeDtypeStruct((T, ATTN_WIDTH), BF16),
        grid=(batch, nq),
        in_specs=[
            pl.BlockSpec((TQ, IDX_HEADS * IDX_DIM), tile),
            pl.BlockSpec((TQ, LANES), tile),
            pl.BlockSpec((TQ, ATTN_WIDTH), tile),
            pl.BlockSpec((seq, LANES), per_batch),
            pl.BlockSpec((seq, LANES), per_batch),
            pl.BlockSpec((seq, LANES), per_batch),
        ],
        out_specs=pl.BlockSpec((TQ, ATTN_WIDTH), tile),
        scratch_shapes=[
            pltpu.VMEM((seq, TQ), F32),
            pltpu.VMEM((1, TQ), F32),
            pltpu.VMEM((IDX_HEADS, TQ, IDX_DIM), BF16),
            pltpu.VMEM((IDX_HEADS, TQ, LANES), F32),
            pltpu.VMEM((KV_HEADS, GROUP * TQ, HEAD_DIM), BF16),
            pltpu.VMEM((seq // KT, KV_HEADS * VT_ROWS, KT), BF16),
            pltpu.VMEM((ATTN_HEADS, 1, TQ), F32),
            pltpu.VMEM((ATTN_HEADS, 1, TQ), F32),
            pltpu.VMEM((ATTN_WIDTH, TQ), F32),
        ],
        compiler_params=pltpu.CompilerParams(
            dimension_semantics=("arbitrary", "arbitrary"), vmem_limit_bytes=VMEM_LIMIT_BYTES),
        name="dsa_attention",
    )(qi, wi, q, ki, k, v)


def _mix_kernel(y_ref, uc_ref, uch_ref, up_ref, uph_ref, h_ref, dw_ref, cb_ref, lg_ref, lb_ref,
                pw_ref, plw_ref, psc_ref, wo_ref, gpost_ref, o_ref, ypad_ref, upad_ref):
    j = pl.program_id(1)
    has_prev = (j > 0).astype(F32)

    def glu(u):
        return u[:, :CONV_WIDTH] * jax.nn.sigmoid(u[:, CONV_WIDTH:])

    ypad_ref[0:HALO, :] = glu(uch_ref[...]) * has_prev
    ypad_ref[HALO:HALO + TM, :] = glu(uc_ref[...])
    acc = None
    for tap in range(CONV_KERNEL):
        off = HALO - (CONV_KERNEL - 1) + tap
        term = ypad_ref[off:off + TM, :] * dw_ref[tap:tap + 1, :]
        acc = term if acc is None else acc + term
    yc = acc + cb_ref[...]
    mu = jnp.mean(yc, axis=-1, keepdims=True)
    var = jnp.mean(jnp.square(yc - mu), axis=-1, keepdims=True)
    yn = (yc - mu) * lax.rsqrt(var + LN_EPS) * lg_ref[...] + lb_ref[...]
    ys = yn * jax.nn.sigmoid(yn)
    y_conv = jnp.dot(ys.astype(BF16), pw_ref[...], preferred_element_type=F32)

    upad_ref[0:HALO, :] = uph_ref[...] * has_prev
    u0 = up_ref[...]
    upad_ref[HALO:HALO + TM, :] = u0

    def back(i):
        return upad_ref[HALO - i:HALO - i + TM, :]

    sums = []
    run = u0
    nxt = 1
    for w in POOL_WINDOWS:
        while nxt < w:
            run = run + back(nxt)
            nxt += 1
        sums.append(run)
    lane = lax.broadcasted_iota(I32, (TM, POOL_WIDTH), 1)
    win_sum = sums[-1]
    win = jnp.full((TM, POOL_WIDTH), POOL_WINDOWS[-1], I32)
    for gi in range(POOL_GROUPS - 2, -1, -1):
        in_group = lane < (gi + 1) * POOL_GROUP_DIM
        win_sum = jnp.where(in_group, sums[gi], win_sum)
        win = jnp.where(in_group, POOL_WINDOWS[gi], win)
    t = lax.broadcasted_iota(I32, (TM, POOL_WIDTH), 0) + j * TM
    count = jnp.minimum(t + 1, win).astype(F32)
    pooled = win_sum / count - u0
    y_pool = jnp.dot(pooled.astype(BF16), plw_ref[...], preferred_element_type=F32) * psc_ref[...]

    c0, c1 = ATTN_WIDTH, ATTN_WIDTH + CONV_WIDTH
    mix = (jnp.dot(y_ref[...], wo_ref[0:c0, :], preferred_element_type=F32)
           + jnp.dot(y_conv.astype(BF16), wo_ref[c0:c1, :], preferred_element_type=F32)
           + jnp.dot(y_pool.astype(BF16), wo_ref[c1:, :], preferred_element_type=F32))
    o_ref[...] = h_ref[...] + _rms(mix, gpost_ref[...])


def _mix(y, uc, up, h, dw, cb, lg, lb, pw, plw, psc, wo, gpost, batch, seq):
    T = batch * seq
    nt = seq // TM
    tile = lambda b, j: (b * nt + j, 0)
    halo_per_tile = TM // HALO
    halo = lambda b, j: (b * (seq // HALO) + jnp.maximum(j * halo_per_tile - 1, 0), 0)
    return pl.pallas_call(
        _mix_kernel,
        out_shape=jax.ShapeDtypeStruct((T, D_MODEL), F32),
        grid=(batch, nt),
        in_specs=[
            pl.BlockSpec((TM, ATTN_WIDTH), tile),
            pl.BlockSpec((TM, 2 * CONV_WIDTH), tile),
            pl.BlockSpec((HALO, 2 * CONV_WIDTH), halo),
            pl.BlockSpec((TM, POOL_WIDTH), tile),
            pl.BlockSpec((HALO, POOL_WIDTH), halo),
            pl.BlockSpec((TM, D_MODEL), tile),
            _const_spec((HALO, CONV_WIDTH)),
            _const_spec((1, CONV_WIDTH)),
            _const_spec((1, CONV_WIDTH)),
            _const_spec((1, CONV_WIDTH)),
            _const_spec((CONV_WIDTH, CONV_WIDTH)),
            _const_spec((POOL_WIDTH, POOL_WIDTH)),
            _const_spec((1, POOL_WIDTH)),
            _const_spec((D_MODEL, D_MODEL)),
            _const_spec((1, D_MODEL)),
        ],
        out_specs=pl.BlockSpec((TM, D_MODEL), tile),
        scratch_shapes=[
            pltpu.VMEM((HALO + TM, CONV_WIDTH), F32),
            pltpu.VMEM((HALO + TM, POOL_WIDTH), F32),
        ],
        compiler_params=pltpu.CompilerParams(
            dimension_semantics=("parallel", "parallel"), vmem_limit_bytes=VMEM_LIMIT_BYTES),
        name="mix_out",
    )(y, uc, uc, up, up, h, dw, cb, lg, lb, pw, plw, psc, wo, gpost)


FF_CHUNK = 1024


def _mlp_kernel(h_ref, p_ref, gpre_ref, wup_ref, wdn_ref, gpost_ref, wg_ref, wp_ref, o_ref):
    h = h_ref[...]
    m = _rms(h, gpre_ref[...]).astype(BF16)
    acc = None
    for c in range(D_FF // FF_CHUNK):
        cs = slice(c * FF_CHUNK, (c + 1) * FF_CHUNK)
        r = jnp.maximum(jnp.dot(m, wup_ref[:, cs], preferred_element_type=F32), 0.0)
        d = jnp.dot((r * r).astype(BF16), wdn_ref[cs, :], preferred_element_type=F32)
        acc = d if acc is None else acc + d
    h2 = h + _rms(acc, gpost_ref[...])
    gate = jax.nn.sigmoid(jnp.dot(h2.astype(BF16), wg_ref[...], preferred_element_type=F32))
    emb = jnp.dot(p_ref[...].astype(BF16), wp_ref[...], preferred_element_type=F32)
    o_ref[...] = h2 + gate * emb


def _mlp(h, p, gpre, wup, wdn, gpost, wg, wp):
    T = h.shape[0]
    row = lambda i: (i, 0)
    return pl.pallas_call(
        _mlp_kernel,
        out_shape=jax.ShapeDtypeStruct((T, D_MODEL), F32),
        grid=(T // TM,),
        in_specs=[
            pl.BlockSpec((TM, D_MODEL), row),
            pl.BlockSpec((TM, PLE_DIM), row),
            _const_spec((1, D_MODEL)),
            _const_spec((D_MODEL, D_FF)),
            _const_spec((D_FF, D_MODEL)),
            _const_spec((1, D_MODEL)),
            _const_spec((D_MODEL, D_MODEL)),
            _const_spec((PLE_DIM, D_MODEL)),
        ],
        out_specs=pl.BlockSpec((TM, D_MODEL), row),
        compiler_params=pltpu.CompilerParams(
            dimension_semantics=("parallel",), vmem_limit_bytes=VMEM_LIMIT_BYTES),
        name="mlp_ple",
    )(h, p, gpre, wup, wdn, gpost, wg, wp)


def _lane_constants():
    inv_freq = ROPE_THETA ** (-jnp.arange(0, ROPE_DIM, 2, dtype=F32) / ROPE_DIM)
    lane = np.arange(LANES) % HEAD_DIM
    rot = lane < ROPE_DIM
    freq = jnp.where(jnp.asarray(rot), inv_freq[lane % ROPE_HALF], 0.0)
    lo = jnp.asarray(np.where(lane < ROPE_HALF, -1.0, 0.0), F32)
    hi = jnp.asarray(np.where(rot & (lane >= ROPE_HALF), 1.0, 0.0), F32)
    return jnp.concatenate([jnp.stack([freq, lo, hi]), jnp.zeros((5, LANES), F32)], axis=0)


def kernel(x, p, positions, norm_mix_pre, w_in, conv_dw, conv_b, conv_ln_g, conv_ln_b, conv_pw,
           pool_w, pool_scale, w_out, norm_mix_post, norm_mlp_pre, w_up, w_down, norm_mlp_post,
           ple_proj, ple_gate):
    batch, seq, d_model = x.shape
    depth = w_in.shape[0]
    assert d_model == D_MODEL and seq % TQ == 0 and seq % TM == 0 and TQ == KT
    T = batch * seq

    w_in_pad = jnp.concatenate(
        [w_in[:, :, :IN_UNPADDED_SPLIT],
         jnp.zeros((depth, D_MODEL, COL_CONV - IN_UNPADDED_SPLIT), w_in.dtype),
         w_in[:, :, IN_UNPADDED_SPLIT:]], axis=2).astype(BF16)
    dw_pad = jnp.concatenate([conv_dw, jnp.zeros((depth, HALO - CONV_KERNEL, CONV_WIDTH), F32)], axis=1)
    pool_bd = jnp.einsum('lgcd,gh->lgchd', pool_w, jnp.eye(POOL_GROUPS, dtype=F32)).reshape(
        depth, POOL_WIDTH, POOL_WIDTH).astype(BF16)
    conv_pw_b = conv_pw.astype(BF16)
    w_out_b = w_out.astype(BF16)
    w_up_b = w_up.astype(BF16)
    w_down_b = w_down.astype(BF16)
    gate_b = ple_gate.astype(BF16)
    proj_b = ple_proj.astype(BF16)
    vec = lambda a, i: a[i].reshape(1, -1)

    lane_consts = _lane_constants()
    pos = positions.reshape(T, 1)
    h = x.reshape(T, D_MODEL)
    p2 = p.reshape(depth, T, PLE_DIM)
    for i in range(depth):
        q, k, v, qi, ki, wi, uc, up = _inproj(h, pos, vec(norm_mix_pre, i), w_in_pad[i], lane_consts)
        y = _attention(qi, wi, q, ki, k, v, batch, seq)
        h = _mix(y, uc, up, h, dw_pad[i], vec(conv_b, i), vec(conv_ln_g, i), vec(conv_ln_b, i),
                 conv_pw_b[i], pool_bd[i], vec(pool_scale, i), w_out_b[i], vec(norm_mix_post, i),
                 batch, seq)
        h = _mlp(h, p2[i], vec(norm_mlp_pre, i), w_up_b[i], w_down_b[i], vec(norm_mlp_post, i),
                 gate_b[i], proj_b[i])
    return h.reshape(batch, seq, D_MODEL)
```

```python
import functools

import numpy as np
import jax
import jax.numpy as jnp
from jax import lax
from jax.experimental import pallas as pl
from jax.experimental.pallas import tpu as pltpu

F32 = jnp.float32
BF16 = jnp.bfloat16
I32 = jnp.int32

D_MODEL = 1024
HEAD_DIM = 64
ATTN_WIDTH = 512
ATTN_HEADS = 8
KV_HEADS = 2
GROUP = ATTN_HEADS // KV_HEADS
IDX_HEADS = 16
IDX_DIM = 64
INDEX_TOPK = 256
CONV_WIDTH = 256
CONV_KERNEL = 31
POOL_WIDTH = 256
POOL_GROUPS = 4
POOL_GROUP_DIM = 64
POOL_WINDOWS = (2, 4, 8, 16)
D_FF = 4096
PLE_DIM = 256
ROPE_THETA = 500000.0
ROPE_DIM = 16
ROPE_HALF = ROPE_DIM // 2
NORM_EPS = 1e-6
LN_EPS = 1e-5
ATT_SCALE = HEAD_DIM ** -0.5
Q_SCALE = ATT_SCALE * float(np.log2(np.e))
IDX_SCALE = (IDX_DIM ** -0.5) * (IDX_HEADS ** -0.5)

LANES = 128
SUBLANES = 8
VMEM_LIMIT_BYTES = 56 * 1024 * 1024

COL_Q = 0
COL_K = COL_Q + ATTN_WIDTH
COL_V = COL_K + KV_HEADS * HEAD_DIM
COL_QI = COL_V + KV_HEADS * HEAD_DIM
COL_KIWI = COL_QI + IDX_HEADS * IDX_DIM
COL_CONV = COL_KIWI + LANES
COL_POOL = COL_CONV + 2 * CONV_WIDTH
IN_PAD_WIDTH = COL_POOL + POOL_WIDTH
IN_UNPADDED_SPLIT = ATTN_WIDTH + 2 * KV_HEADS * HEAD_DIM + IDX_HEADS * IDX_DIM + IDX_DIM + IDX_HEADS
WI_LANE0 = IDX_DIM

TM = 512
TQ = 256
KT = 256
QH = 128
HALO = 32
SEARCH_UNROLL = 4
COUNT_CHAINS = 8
SEARCH_ROUNDS = 12
VT_ROWS = HEAD_DIM + 16

NEG_MASK = -1e30
M_INIT = -2e30
INT_MIN = -2 ** 31


def _const_spec(shape):
    zeros = (0,) * len(shape)
    return pl.BlockSpec(shape, lambda *_: zeros, pipeline_mode=pl.Buffered(1))


def _layer_spec(layer, shape):
    index = (layer,) + (0,) * len(shape)
    return pl.BlockSpec((None,) + tuple(shape), lambda *_: index, pipeline_mode=pl.Buffered(1))


def _rms(x, gain):
    return x * lax.rsqrt(jnp.mean(x * x, axis=-1, keepdims=True) + NORM_EPS) * gain


def _inproj_kernel(h_ref, pos_ref, g_ref, w_ref, lc_ref,
                   q_ref, k_ref, v_ref, qi_ref, ki_ref, wi_ref, uc_ref, up_ref):
    a = _rms(h_ref[...], g_ref[...]).astype(BF16)
    pos = pos_ref[...].astype(F32)
    ang = pos * lc_ref[0:1, :]
    cos = jnp.cos(ang)
    sin = jnp.sin(ang)
    s_lo = sin * lc_ref[1:2, :]
    s_hi = sin * lc_ref[2:3, :]

    def rope(x):
        return x * cos + pltpu.roll(x, LANES - ROPE_HALF, 1) * s_lo + pltpu.roll(x, ROPE_HALF, 1) * s_hi

    def proj(c0, c1):
        return jnp.dot(a, w_ref[:, c0:c1], preferred_element_type=F32)

    u = proj(COL_Q, COL_K)
    for s in range(ATTN_WIDTH // LANES):
        sl = slice(s * LANES, (s + 1) * LANES)
        q_ref[:, sl] = (rope(u[:, sl]) * Q_SCALE).astype(BF16)
    k_ref[...] = rope(proj(COL_K, COL_V)).astype(BF16)
    v_ref[...] = proj(COL_V, COL_QI).astype(BF16)
    u = proj(COL_QI, COL_KIWI)
    for s in range(IDX_HEADS * IDX_DIM // LANES):
        sl = slice(s * LANES, (s + 1) * LANES)
        qi_ref[:, sl] = rope(u[:, sl]).astype(BF16)
    u = proj(COL_KIWI, COL_CONV)
    ki_ref[...] = rope(u).astype(BF16)
    wi_ref[...] = u * IDX_SCALE
    uc_ref[...] = proj(COL_CONV, COL_POOL)
    up_ref[...] = proj(COL_POOL, IN_PAD_WIDTH)


def _inproj(layer, h, pos, gain, w, lane_consts):
    T = h.shape[0]
    row = lambda i: (i, 0)
    out_shape = (
        jax.ShapeDtypeStruct((T, ATTN_WIDTH), BF16),
        jax.ShapeDtypeStruct((T, LANES), BF16),
        jax.ShapeDtypeStruct((T, LANES), BF16),
        jax.ShapeDtypeStruct((T, IDX_HEADS * IDX_DIM), BF16),
        jax.ShapeDtypeStruct((T, LANES), BF16),
        jax.ShapeDtypeStruct((T, LANES), F32),
        jax.ShapeDtypeStruct((T, 2 * CONV_WIDTH), F32),
        jax.ShapeDtypeStruct((T, POOL_WIDTH), F32),
    )
    return pl.pallas_call(
        _inproj_kernel,
        out_shape=out_shape,
        grid=(T // TM,),
        in_specs=[
            pl.BlockSpec((TM, D_MODEL), row),
            pl.BlockSpec((TM, 1), row),
            _const_spec((1, D_MODEL)),
            _layer_spec(layer, (D_MODEL, IN_PAD_WIDTH)),
            _const_spec((8, LANES)),
        ],
        out_specs=tuple(pl.BlockSpec((TM, s.shape[1]), row) for s in out_shape),
        compiler_params=pltpu.CompilerParams(
            dimension_semantics=("parallel",), vmem_limit_bytes=VMEM_LIMIT_BYTES),
        name="inproj",
    )(h, pos, gain, w, lane_consts)


_NT = (((1,), (1,)), ((), ()))


def _key_rows(kt):
    return pl.ds(pl.multiple_of(kt * KT, KT), KT)


def _sortable(x):
    bits = pltpu.bitcast(x, I32)
    return bits ^ ((bits >> 31) & 0x7FFFFFFF)


def _indexer(nkt, q_start, qi_ref, wi_ref, ki_ref, idx_ref, qis_ref, wb_ref):
    for hh in range(IDX_HEADS):
        qis_ref[hh] = qi_ref[:, hh * IDX_DIM:(hh + 1) * IDX_DIM]
        wb_ref[hh] = jnp.broadcast_to(wi_ref[:, WI_LANE0 + hh:WI_LANE0 + hh + 1], (TQ, LANES))

    def key_tile(kt, carry):
        ki = ki_ref[_key_rows(kt), 0:IDX_DIM]
        for qh in range(TQ // QH):
            qs = slice(qh * QH, (qh + 1) * QH)
            acc = [None] * (KT // LANES)
            for hh in range(IDX_HEADS):
                d = lax.dot_general(qis_ref[hh, qs, :], ki, _NT, preferred_element_type=F32)
                w = wb_ref[hh, qs, :]
                for c in range(KT // LANES):
                    term = jnp.maximum(d[:, c * LANES:(c + 1) * LANES], 0.0) * w
                    acc[c] = term if acc[c] is None else acc[c] + term
            qpos = lax.broadcasted_iota(I32, (QH, LANES), 0) + (q_start + qh * QH)
            for c in range(KT // LANES):
                kpos = lax.broadcasted_iota(I32, (QH, LANES), 1) + (kt * KT + c * LANES)
                blk = jnp.where(kpos <= qpos, acc[c], -jnp.inf)
                rows = pl.ds(pl.multiple_of(kt * KT + c * LANES, LANES), LANES)
                idx_ref[rows, qs] = blk.T
        return carry

    lax.fori_loop(0, nkt, key_tile, 0)


def _over_key_tiles(nkt, idx_ref, fn, init):
    return lax.fori_loop(0, nkt, lambda kt, c: fn(idx_ref[_key_rows(kt), :], c), init)


def _select(nkt, k_sel, idx_ref, lo_ref):
    kf = float(k_sel)

    def stats(x, c):
        cmax, cmin, cfin = c
        fin = x > -jnp.inf
        return (jnp.maximum(cmax, jnp.max(x, axis=0, keepdims=True)),
                jnp.minimum(cmin, jnp.min(jnp.where(fin, x, jnp.inf), axis=0, keepdims=True)),
                cfin + jnp.sum(jnp.where(fin, 1.0, 0.0), axis=0, keepdims=True))

    row = lambda v: jnp.full((1, TQ), v, F32)
    cmax, cmin, cfin = _over_key_tiles(nkt, idx_ref, stats, (row(-jnp.inf), row(jnp.inf), row(0.0)))

    @pl.when(nkt % 2 == 1)
    def _():
        idx_ref[_key_rows(nkt), :] = jnp.full((KT, TQ), -jnp.inf, F32)
    npairs = (nkt + 1) // 2

    def count_ge(t):
        tb = jnp.broadcast_to(t, (8, TQ))

        def pair(i, acc):
            tiles = idx_ref.at[pl.ds(pl.multiple_of(i * (2 * KT), 2 * KT), 2 * KT), :]
            chains = [None] * COUNT_CHAINS
            for r in range(2 * KT // 8):
                hit = jnp.where(tiles[r * 8:(r + 1) * 8, :] >= tb, 1.0, 0.0)
                c = r % COUNT_CHAINS
                chains[c] = hit if chains[c] is None else chains[c] + hit
            while len(chains) > 1:
                chains = [chains[a] + chains[a + 1] for a in range(0, len(chains), 2)]
            return acc + chains[0]
        acc = lax.fori_loop(0, npairs, pair, jnp.zeros((8, TQ), F32))
        return jnp.sum(acc, axis=0, keepdims=True)

    def step(state):
        lo, hi, cnt_lo = state
        mid = lo + (hi - lo) * 0.5
        cnt = count_ge(mid)
        up = cnt >= kf
        return jnp.where(up, mid, lo), jnp.where(up, hi, mid), jnp.where(up, cnt, cnt_lo)

    def unsettled(cnt_lo):
        return jnp.max(jnp.where(cnt_lo != kf, 1.0, 0.0))

    def cond(c):
        return (c[0] < SEARCH_ROUNDS) & (c[1] > 0.0)

    def body(c):
        state = c[2:]
        for _ in range(SEARCH_UNROLL):
            state = step(state)
        return (c[0] + 1, unsettled(state[2])) + state

    out = lax.while_loop(cond, body, (jnp.int32(0), unsettled(cfin), cmin, cmax, cfin))
    lo_ref[...] = out[2]
    return out[1]


def _select_exact(nkt, k_sel, idx_ref, lo_ref):
    def count(pred):
        return _over_key_tiles(
            nkt, idx_ref,
            lambda x, c: c + jnp.sum(pred(_sortable(x)).astype(I32), axis=0, keepdims=True),
            jnp.zeros((1, TQ), I32))

    def search(i, prefix):
        cand = prefix | lax.shift_left(jnp.int32(1), 31 - i)
        return jnp.where(count(lambda key: key >= (cand ^ INT_MIN)) >= k_sel, cand, prefix)

    thr = lax.fori_loop(0, 32, search, jnp.zeros((1, TQ), I32)) ^ INT_MIN
    need = (k_sel - count(lambda key: key > thr)).astype(F32)
    r = lax.broadcasted_iota(I32, (KT, KT), 0)
    c = lax.broadcasted_iota(I32, (KT, KT), 1)
    earlier = (c < r).astype(BF16)

    def mark(kt, carry):
        x = idx_ref[_key_rows(kt), :]
        key = _sortable(x)
        eq = key == thr
        eqf = jnp.where(eq, 1.0, 0.0)
        rank = jnp.dot(earlier, eqf.astype(BF16), preferred_element_type=F32) + carry
        keep = ((key > thr) | (eq & (rank < need))) & (x > -jnp.inf)
        idx_ref[_key_rows(kt), :] = jnp.where(keep, 1.0, -jnp.inf)
        return carry + jnp.sum(eqf, axis=0, keepdims=True)

    lax.fori_loop(0, nkt, mark, jnp.zeros((1, TQ), F32))
    lo_ref[...] = jnp.full((1, TQ), 0.5, F32)


def _attend(nkt, q_ref, k_ref, vt_ref, idx_ref, lo_ref, qs_ref, m_ref, l_ref, o_ref, y_ref):
    for hh in range(ATTN_HEADS):
        g, jj = divmod(hh, GROUP)
        qs_ref[g, jj * TQ:(jj + 1) * TQ, :] = q_ref[:, hh * HEAD_DIM:(hh + 1) * HEAD_DIM]
    m_ref[...] = jnp.full(m_ref.shape, M_INIT, F32)
    l_ref[...] = jnp.zeros(l_ref.shape, F32)
    o_ref[...] = jnp.zeros(o_ref.shape, F32)
    lo = lo_ref[...]

    def key_tile(kt, carry):
        rows = _key_rows(kt)
        bias = jnp.where(idx_ref[rows, :] >= lo, 0.0, NEG_MASK)
        scores = [lax.dot_general(k_ref[rows, g * HEAD_DIM:(g + 1) * HEAD_DIM], qs_ref[g], _NT,
                                  preferred_element_type=F32) for g in range(KV_HEADS)]
        for g in range(KV_HEADS):
            probs, alphas = [], []
            for jj in range(GROUP):
                hh = g * GROUP + jj
                s = scores[g][:, jj * TQ:(jj + 1) * TQ] + bias
                m_old = m_ref[hh]
                m_new = jnp.maximum(m_old, jnp.max(s, axis=0, keepdims=True))
                alphas.append(jnp.exp2(m_old - m_new))
                probs.append(jnp.exp2(s - m_new).astype(BF16))
                m_ref[hh] = m_new
            pv = jnp.dot(vt_ref[kt, g * VT_ROWS:(g + 1) * VT_ROWS, :], jnp.concatenate(probs, axis=1),
                         preferred_element_type=F32)
            for jj in range(GROUP):
                hh = g * GROUP + jj
                cols = slice(jj * TQ, (jj + 1) * TQ)
                orow = slice(hh * HEAD_DIM, (hh + 1) * HEAD_DIM)
                o_ref[orow, :] = o_ref[orow, :] * alphas[jj] + pv[0:HEAD_DIM, cols]
                l_ref[hh] = l_ref[hh] * alphas[jj] + pv[HEAD_DIM:HEAD_DIM + 1, cols]
        return carry

    lax.fori_loop(0, nkt, key_tile, 0)
    for hh in range(ATTN_HEADS):
        orow = slice(hh * HEAD_DIM, (hh + 1) * HEAD_DIM)
        o_ref[orow, :] = o_ref[orow, :] / l_ref[hh]
    y_ref[...] = o_ref[...].T.astype(BF16)


def _attn_kernel(seq, k_sel, qi_ref, wi_ref, q_ref, ki_ref, k_ref, v_ref, y_ref,
                 idx_ref, lo_ref, qis_ref, wb_ref, qs_ref, vt_ref, m_ref, l_ref, o_ref):
    j = pl.program_id(1)
    nkt = j + 1
    q_start = j * TQ

    @pl.when(j == 0)
    def _():
        for kt in range(seq // KT):
            v_t = v_ref[kt * KT:(kt + 1) * KT, :].astype(F32).T.astype(BF16)
            for g in range(KV_HEADS):
                vt_ref[kt, g * VT_ROWS:g * VT_ROWS + HEAD_DIM, :] = v_t[g * HEAD_DIM:(g + 1) * HEAD_DIM, :]
                vt_ref[kt, g * VT_ROWS + HEAD_DIM:(g + 1) * VT_ROWS, :] = jnp.ones(
                    (VT_ROWS - HEAD_DIM, KT), BF16)

    all_selected = (j + 1) * TQ <= k_sel

    @pl.when(all_selected)
    def _():
        def fill(kt, carry):
            kpos = lax.broadcasted_iota(I32, (KT, TQ), 0) + kt * KT
            qpos = lax.broadcasted_iota(I32, (KT, TQ), 1) + q_start
            idx_ref[_key_rows(kt), :] = jnp.where(kpos <= qpos, 1.0, -jnp.inf)
            return carry
        lax.fori_loop(0, nkt, fill, 0)
        lo_ref[...] = jnp.full((1, TQ), 0.5, F32)

    @pl.when(jnp.logical_not(all_selected))
    def _():
        _indexer(nkt, q_start, qi_ref, wi_ref, ki_ref, idx_ref, qis_ref, wb_ref)
        failed = _select(nkt, k_sel, idx_ref, lo_ref)

        @pl.when(failed > 0.0)
        def _():
            _select_exact(nkt, k_sel, idx_ref, lo_ref)

    _attend(nkt, q_ref, k_ref, vt_ref, idx_ref, lo_ref, qs_ref, m_ref, l_ref, o_ref, y_ref)


def _attention(qi, wi, q, ki, k, v, batch, seq):
    T = batch * seq
    nq = seq // TQ
    k_sel = min(INDEX_TOPK, seq // 4)
    tile = lambda b, j: (b * nq + j, 0)
    per_batch = lambda b, j: (b, 0)
    return pl.pallas_call(
        functools.partial(_attn_kernel, seq, k_sel),
        out_shape=jax.ShapeDtypeStruct((T, ATTN_WIDTH), BF16),
        grid=(batch, nq),
        in_specs=[
            pl.BlockSpec((TQ, IDX_HEADS * IDX_DIM), tile),
            pl.BlockSpec((TQ, LANES), tile),
            pl.BlockSpec((TQ, ATTN_WIDTH), tile),
            pl.BlockSpec((seq, LANES), per_batch),
            pl.BlockSpec((seq, LANES), per_batch),
            pl.BlockSpec((seq, LANES), per_batch),
        ],
        out_specs=pl.BlockSpec((TQ, ATTN_WIDTH), tile),
        scratch_shapes=[
            pltpu.VMEM((seq, TQ), F32),
            pltpu.VMEM((1, TQ), F32),
            pltpu.VMEM((IDX_HEADS, TQ, IDX_DIM), BF16),
            pltpu.VMEM((IDX_HEADS, TQ, LANES), F32),
            pltpu.VMEM((KV_HEADS, GROUP * TQ, HEAD_DIM), BF16),
            pltpu.VMEM((seq // KT, KV_HEADS * VT_ROWS, KT), BF16),
            pltpu.VMEM((ATTN_HEADS, 1, TQ), F32),
            pltpu.VMEM((ATTN_HEADS, 1, TQ), F32),
            pltpu.VMEM((ATTN_WIDTH, TQ), F32),
        ],
        compiler_params=pltpu.CompilerParams(
            dimension_semantics=("arbitrary", "arbitrary"), vmem_limit_bytes=VMEM_LIMIT_BYTES),
        name="dsa_attention",
    )(qi, wi, q, ki, k, v)


def _mix_kernel(y_ref, uc_ref, uch_ref, up_ref, uph_ref, h_ref, dw_ref, cb_ref, lg_ref, lb_ref,
                pw_ref, plw_ref, psc_ref, wo_ref, gpost_ref, o_ref, ypad_ref, upad_ref):
    j = pl.program_id(1)
    has_prev = (j > 0).astype(F32)

    def glu(u):
        return u[:, :CONV_WIDTH] * jax.nn.sigmoid(u[:, CONV_WIDTH:])

    def fill_shifted(ref, halo_rows, tile_rows):
        ref[0, 0:HALO, :] = halo_rows
        ref[0, HALO:HALO + TM, :] = tile_rows
        for s in range(1, SUBLANES):
            ref[s, 0:HALO + TM - SUBLANES, :] = ref[0, s:s + HALO + TM - SUBLANES, :]

    def rows_from(ref, off):
        s = off % SUBLANES
        return ref[s, off - s:off - s + TM, :]

    fill_shifted(ypad_ref, glu(uch_ref[...]) * has_prev, glu(uc_ref[...]))
    acc = None
    for tap in range(CONV_KERNEL):
        term = rows_from(ypad_ref, HALO - (CONV_KERNEL - 1) + tap) * dw_ref[tap:tap + 1, :]
        acc = term if acc is None else acc + term
    yc = acc + cb_ref[...]
    mu = jnp.mean(yc, axis=-1, keepdims=True)
    var = jnp.mean(jnp.square(yc - mu), axis=-1, keepdims=True)
    yn = (yc - mu) * lax.rsqrt(var + LN_EPS) * lg_ref[...] + lb_ref[...]
    ys = yn * jax.nn.sigmoid(yn)
    y_conv = jnp.dot(ys.astype(BF16), pw_ref[...], preferred_element_type=F32)

    u0 = up_ref[...]
    fill_shifted(upad_ref, uph_ref[...] * has_prev, u0)

    def back(i):
        return rows_from(upad_ref, HALO - i)

    sums = []
    run = u0
    nxt = 1
    for w in POOL_WINDOWS:
        while nxt < w:
            run = run + back(nxt)
            nxt += 1
        sums.append(run)
    lane = lax.broadcasted_iota(I32, (TM, POOL_WIDTH), 1)
    win_sum = sums[-1]
    win = jnp.full((TM, POOL_WIDTH), POOL_WINDOWS[-1], I32)
    for gi in range(POOL_GROUPS - 2, -1, -1):
        in_group = lane < (gi + 1) * POOL_GROUP_DIM
        win_sum = jnp.where(in_group, sums[gi], win_sum)
        win = jnp.where(in_group, POOL_WINDOWS[gi], win)
    t = lax.broadcasted_iota(I32, (TM, POOL_WIDTH), 0) + j * TM
    count = jnp.minimum(t + 1, win).astype(F32)
    pooled = win_sum / count - u0
    y_pool = jnp.dot(pooled.astype(BF16), plw_ref[...], preferred_element_type=F32) * psc_ref[...]

    c0, c1 = ATTN_WIDTH, ATTN_WIDTH + CONV_WIDTH
    mix = (jnp.dot(y_ref[...], wo_ref[0:c0, :], preferred_element_type=F32)
           + jnp.dot(y_conv.astype(BF16), wo_ref[c0:c1, :], preferred_element_type=F32)
           + jnp.dot(y_pool.astype(BF16), wo_ref[c1:, :], preferred_element_type=F32))
    o_ref[...] = h_ref[...] + _rms(mix, gpost_ref[...])


def _mix(layer, y, uc, up, h, dw, cb, lg, lb, pw, plw, psc, wo, gpost, batch, seq):
    T = batch * seq
    nt = seq // TM
    tile = lambda b, j: (b * nt + j, 0)
    halo_per_tile = TM // HALO
    halo = lambda b, j: (b * (seq // HALO) + jnp.maximum(j * halo_per_tile - 1, 0), 0)
    return pl.pallas_call(
        _mix_kernel,
        out_shape=jax.ShapeDtypeStruct((T, D_MODEL), F32),
        grid=(batch, nt),
        in_specs=[
            pl.BlockSpec((TM, ATTN_WIDTH), tile),
            pl.BlockSpec((TM, 2 * CONV_WIDTH), tile),
            pl.BlockSpec((HALO, 2 * CONV_WIDTH), halo),
            pl.BlockSpec((TM, POOL_WIDTH), tile),
            pl.BlockSpec((HALO, POOL_WIDTH), halo),
            pl.BlockSpec((TM, D_MODEL), tile),
            _const_spec((HALO, CONV_WIDTH)),
            _const_spec((1, CONV_WIDTH)),
            _const_spec((1, CONV_WIDTH)),
            _const_spec((1, CONV_WIDTH)),
            _layer_spec(layer, (CONV_WIDTH, CONV_WIDTH)),
            _layer_spec(layer, (POOL_WIDTH, POOL_WIDTH)),
            _const_spec((1, POOL_WIDTH)),
            _layer_spec(layer, (D_MODEL, D_MODEL)),
            _const_spec((1, D_MODEL)),
        ],
        out_specs=pl.BlockSpec((TM, D_MODEL), tile),
        scratch_shapes=[
            pltpu.VMEM((SUBLANES, HALO + TM, CONV_WIDTH), F32),
            pltpu.VMEM((SUBLANES, HALO + TM, POOL_WIDTH), F32),
        ],
        compiler_params=pltpu.CompilerParams(
            dimension_semantics=("parallel", "parallel"), vmem_limit_bytes=VMEM_LIMIT_BYTES),
        name="mix_out",
    )(y, uc, uc, up, up, h, dw, cb, lg, lb, pw, plw, psc, wo, gpost)


FF_CHUNK = 1024


def _mlp_kernel(h_ref, p_ref, gpre_ref, wup_ref, wdn_ref, gpost_ref, wg_ref, wp_ref, o_ref):
    h = h_ref[...]
    m = _rms(h, gpre_ref[...]).astype(BF16)
    acc = None
    for c in range(D_FF // FF_CHUNK):
        cs = slice(c * FF_CHUNK, (c + 1) * FF_CHUNK)
        r = jnp.maximum(jnp.dot(m, wup_ref[:, cs], preferred_element_type=F32), 0.0)
        d = jnp.dot((r * r).astype(BF16), wdn_ref[cs, :], preferred_element_type=F32)
        acc = d if acc is None else acc + d
    h2 = h + _rms(acc, gpost_ref[...])
    gate = jax.nn.sigmoid(jnp.dot(h2.astype(BF16), wg_ref[...], preferred_element_type=F32))
    emb = jnp.dot(p_ref[...].astype(BF16), wp_ref[...], preferred_element_type=F32)
    o_ref[...] = h2 + gate * emb


def _mlp(layer, h, p, gpre, wup, wdn, gpost, wg, wp):
    T = h.shape[0]
    row = lambda i: (i, 0)
    return pl.pallas_call(
        _mlp_kernel,
        out_shape=jax.ShapeDtypeStruct((T, D_MODEL), F32),
        grid=(T // TM,),
        in_specs=[
            pl.BlockSpec((TM, D_MODEL), row),
            pl.BlockSpec((None, TM, PLE_DIM), lambda i: (layer, i, 0)),
            _const_spec((1, D_MODEL)),
            _layer_spec(layer, (D_MODEL, D_FF)),
            _layer_spec(layer, (D_FF, D_MODEL)),
            _const_spec((1, D_MODEL)),
            _layer_spec(layer, (D_MODEL, D_MODEL)),
            _layer_spec(layer, (PLE_DIM, D_MODEL)),
        ],
        out_specs=pl.BlockSpec((TM, D_MODEL), row),
        compiler_params=pltpu.CompilerParams(
            dimension_semantics=("parallel",), vmem_limit_bytes=VMEM_LIMIT_BYTES),
        name="mlp_ple",
    )(h, p, gpre, wup, wdn, gpost, wg, wp)


def _lane_constants():
    inv_freq = ROPE_THETA ** (-jnp.arange(0, ROPE_DIM, 2, dtype=F32) / ROPE_DIM)
    lane = np.arange(LANES) % HEAD_DIM
    rot = lane < ROPE_DIM
    freq = jnp.where(jnp.asarray(rot), inv_freq[lane % ROPE_HALF], 0.0)
    lo = jnp.asarray(np.where(lane < ROPE_HALF, -1.0, 0.0), F32)
    hi = jnp.asarray(np.where(rot & (lane >= ROPE_HALF), 1.0, 0.0), F32)
    return jnp.concatenate([jnp.stack([freq, lo, hi]), jnp.zeros((5, LANES), F32)], axis=0)


def kernel(x, p, positions, norm_mix_pre, w_in, conv_dw, conv_b, conv_ln_g, conv_ln_b, conv_pw,
           pool_w, pool_scale, w_out, norm_mix_post, norm_mlp_pre, w_up, w_down, norm_mlp_post,
           ple_proj, ple_gate):
    batch, seq, d_model = x.shape
    depth = w_in.shape[0]
    assert d_model == D_MODEL and seq % TQ == 0 and seq % TM == 0 and TQ == KT
    T = batch * seq

    w_in_pad = jnp.concatenate(
        [w_in[:, :, :IN_UNPADDED_SPLIT],
         jnp.zeros((depth, D_MODEL, COL_CONV - IN_UNPADDED_SPLIT), w_in.dtype),
         w_in[:, :, IN_UNPADDED_SPLIT:]], axis=2).astype(BF16)
    dw_pad = jnp.concatenate([conv_dw, jnp.zeros((depth, HALO - CONV_KERNEL, CONV_WIDTH), F32)], axis=1)
    pool_bd = jnp.einsum('lgcd,gh->lgchd', pool_w, jnp.eye(POOL_GROUPS, dtype=F32)).reshape(
        depth, POOL_WIDTH, POOL_WIDTH).astype(BF16)
    conv_pw_b = conv_pw.astype(BF16)
    w_out_b = w_out.astype(BF16)
    w_up_b = w_up.astype(BF16)
    w_down_b = w_down.astype(BF16)
    gate_b = ple_gate.astype(BF16)
    proj_b = ple_proj.astype(BF16)
    vec = lambda a, i: a[i].reshape(1, -1)

    lane_consts = _lane_constants()
    pos = positions.reshape(T, 1)
    h = x.reshape(T, D_MODEL)
    p2 = p.reshape(depth, T, PLE_DIM)
    for i in range(depth):
        q, k, v, qi, ki, wi, uc, up = _inproj(i, h, pos, vec(norm_mix_pre, i), w_in_pad, lane_consts)
        y = _attention(qi, wi, q, ki, k, v, batch, seq)
        h = _mix(i, y, uc, up, h, dw_pad[i], vec(conv_b, i), vec(conv_ln_g, i), vec(conv_ln_b, i),
                 conv_pw_b, pool_bd, vec(pool_scale, i), w_out_b, vec(norm_mix_post, i), batch, seq)
        h = _mlp(i, h, p2, vec(norm_mlp_pre, i), w_up_b, w_down_b, vec(norm_mlp_post, i),
                 gate_b, proj_b)
    return h.reshape(batch, seq, D_MODEL)
```

```python
import collections
import functools

import numpy as np
import jax
import jax.numpy as jnp
from jax import lax
from jax.experimental import pallas as pl
from jax.experimental.pallas import tpu as pltpu

F32 = jnp.float32
BF16 = jnp.bfloat16
I32 = jnp.int32

D_MODEL = 1024
HEAD_DIM = 64
ATTN_WIDTH = 512
ATTN_HEADS = 8
KV_HEADS = 2
GROUP = ATTN_HEADS // KV_HEADS
IDX_HEADS = 16
IDX_DIM = 64
INDEX_TOPK = 256
CONV_WIDTH = 256
CONV_KERNEL = 31
POOL_WIDTH = 256
POOL_GROUPS = 4
POOL_GROUP_DIM = 64
POOL_WINDOWS = (2, 4, 8, 16)
D_FF = 4096
PLE_DIM = 256
ROPE_THETA = 500000.0
ROPE_DIM = 16
ROPE_HALF = ROPE_DIM // 2
NORM_EPS = 1e-6
LN_EPS = 1e-5
ATT_SCALE = HEAD_DIM ** -0.5
Q_SCALE = ATT_SCALE * float(np.log2(np.e))
IDX_SCALE = (IDX_DIM ** -0.5) * (IDX_HEADS ** -0.5)

LANES = 128
SUBLANES = 8
VMEM_LIMIT_BYTES = 56 * 1024 * 1024

COL_Q = 0
COL_K = COL_Q + ATTN_WIDTH
COL_V = COL_K + KV_HEADS * HEAD_DIM
COL_QI = COL_V + KV_HEADS * HEAD_DIM
COL_KIWI = COL_QI + IDX_HEADS * IDX_DIM
COL_CONV = COL_KIWI + LANES
COL_POOL = COL_CONV + 2 * CONV_WIDTH
IN_PAD_WIDTH = COL_POOL + POOL_WIDTH
IN_UNPADDED_SPLIT = ATTN_WIDTH + 2 * KV_HEADS * HEAD_DIM + IDX_HEADS * IDX_DIM + IDX_DIM + IDX_HEADS
WI_LANE0 = IDX_DIM

TM = 512
TQ = 256
KT = 256
QH = 128
SEQ_TILE = 2
HALO = 32
SEARCH_UNROLL = 4
COUNT_CHAINS = 8
SEARCH_ROUNDS = 12
VT_ROWS = HEAD_DIM + 16

NEG_MASK = -1e30
M_INIT = -2e30
INT_MIN = -2 ** 31


def _const_spec(shape):
    zeros = (0,) * len(shape)
    return pl.BlockSpec(shape, lambda *_: zeros, pipeline_mode=pl.Buffered(1))


def _layer_spec(layer, shape):
    index = (layer,) + (0,) * len(shape)
    return pl.BlockSpec((None,) + tuple(shape), lambda *_: index, pipeline_mode=pl.Buffered(1))


def _rms(x, gain):
    return x * lax.rsqrt(jnp.mean(x * x, axis=-1, keepdims=True) + NORM_EPS) * gain


def _inproj_kernel(h_ref, pos_ref, g_ref, w_ref, lc_ref,
                   q_ref, k_ref, v_ref, qi_ref, ki_ref, wi_ref, uc_ref, up_ref):
    a = _rms(h_ref[...], g_ref[...]).astype(BF16)
    pos = pos_ref[...].astype(F32)
    ang = pos * lc_ref[0:1, :]
    cos = jnp.cos(ang)
    sin = jnp.sin(ang)
    s_lo = sin * lc_ref[1:2, :]
    s_hi = sin * lc_ref[2:3, :]

    def rope(x):
        return x * cos + pltpu.roll(x, LANES - ROPE_HALF, 1) * s_lo + pltpu.roll(x, ROPE_HALF, 1) * s_hi

    def proj(c0, c1):
        return jnp.dot(a, w_ref[:, c0:c1], preferred_element_type=F32)

    u = proj(COL_Q, COL_K)
    for s in range(ATTN_WIDTH // LANES):
        sl = slice(s * LANES, (s + 1) * LANES)
        q_ref[:, sl] = (rope(u[:, sl]) * Q_SCALE).astype(BF16)
    k_ref[...] = rope(proj(COL_K, COL_V)).astype(BF16)
    v_ref[...] = proj(COL_V, COL_QI).astype(BF16)
    u = proj(COL_QI, COL_KIWI)
    for s in range(IDX_HEADS * IDX_DIM // LANES):
        sl = slice(s * LANES, (s + 1) * LANES)
        qi_ref[:, sl] = rope(u[:, sl]).astype(BF16)
    u = proj(COL_KIWI, COL_CONV)
    ki_ref[...] = rope(u).astype(BF16)
    wi_ref[...] = u * IDX_SCALE
    uc_ref[...] = proj(COL_CONV, COL_POOL)
    up_ref[...] = proj(COL_POOL, IN_PAD_WIDTH)


def _inproj(layer, h, pos, gain, w, lane_consts):
    T = h.shape[0]
    row = lambda i: (i, 0)
    out_shape = (
        jax.ShapeDtypeStruct((T, ATTN_WIDTH), BF16),
        jax.ShapeDtypeStruct((T, LANES), BF16),
        jax.ShapeDtypeStruct((T, LANES), BF16),
        jax.ShapeDtypeStruct((T, IDX_HEADS * IDX_DIM), BF16),
        jax.ShapeDtypeStruct((T, LANES), BF16),
        jax.ShapeDtypeStruct((T, LANES), F32),
        jax.ShapeDtypeStruct((T, 2 * CONV_WIDTH), F32),
        jax.ShapeDtypeStruct((T, POOL_WIDTH), F32),
    )
    return pl.pallas_call(
        _inproj_kernel,
        out_shape=out_shape,
        grid=(T // TM,),
        in_specs=[
            pl.BlockSpec((TM, D_MODEL), row),
            pl.BlockSpec((TM, 1), row),
            _const_spec((1, D_MODEL)),
            _layer_spec(layer, (D_MODEL, IN_PAD_WIDTH)),
            _const_spec((8, LANES)),
        ],
        out_specs=tuple(pl.BlockSpec((TM, s.shape[1]), row) for s in out_shape),
        compiler_params=pltpu.CompilerParams(
            dimension_semantics=("parallel",), vmem_limit_bytes=VMEM_LIMIT_BYTES),
        name="inproj",
    )(h, pos, gain, w, lane_consts)


_NT = (((1,), (1,)), ((), ()))

_Seq = collections.namedtuple(
    "_Seq", "qi wi q ki k v y idx lo qis wb qs vt m l o")


def _key_rows(kt):
    return pl.ds(pl.multiple_of(kt * KT, KT), KT)


def _sortable(x):
    bits = pltpu.bitcast(x, I32)
    return bits ^ ((bits >> 31) & 0x7FFFFFFF)


def _indexer(nkt, q_start, seqs):
    for sq in seqs:
        for hh in range(IDX_HEADS):
            sq.qis[hh] = sq.qi[:, hh * IDX_DIM:(hh + 1) * IDX_DIM]
            sq.wb[hh] = jnp.broadcast_to(sq.wi[:, WI_LANE0 + hh:WI_LANE0 + hh + 1], (TQ, LANES))

    def key_tile(kt, carry):
        out = []
        for sq, (cmax, cmin) in zip(seqs, carry):
            ki = sq.ki[_key_rows(kt), 0:IDX_DIM]
            tmax, tmin = [], []
            for qh in range(TQ // QH):
                qs = slice(qh * QH, (qh + 1) * QH)
                acc = [None] * (KT // LANES)
                for hh in range(IDX_HEADS):
                    d = lax.dot_general(sq.qis[hh, qs, :], ki, _NT, preferred_element_type=F32)
                    w = sq.wb[hh, qs, :]
                    for c in range(KT // LANES):
                        term = jnp.maximum(d[:, c * LANES:(c + 1) * LANES], 0.0) * w
                        acc[c] = term if acc[c] is None else acc[c] + term
                qpos = lax.broadcasted_iota(I32, (QH, LANES), 0) + (q_start + qh * QH)
                hmax, hmin = None, None
                for c in range(KT // LANES):
                    kpos = lax.broadcasted_iota(I32, (QH, LANES), 1) + (kt * KT + c * LANES)
                    blk = jnp.where(kpos <= qpos, acc[c], -jnp.inf).T
                    rows = pl.ds(pl.multiple_of(kt * KT + c * LANES, LANES), LANES)
                    sq.idx[rows, qs] = blk
                    bmax = jnp.max(blk, axis=0, keepdims=True)
                    bmin = jnp.min(jnp.where(blk > -jnp.inf, blk, jnp.inf), axis=0, keepdims=True)
                    hmax = bmax if hmax is None else jnp.maximum(hmax, bmax)
                    hmin = bmin if hmin is None else jnp.minimum(hmin, bmin)
                tmax.append(hmax)
                tmin.append(hmin)
            out.append((jnp.maximum(cmax, jnp.concatenate(tmax, axis=1)),
                        jnp.minimum(cmin, jnp.concatenate(tmin, axis=1))))
        return tuple(out)

    row = lambda v: jnp.full((1, TQ), v, F32)
    return lax.fori_loop(0, nkt, key_tile, tuple((row(-jnp.inf), row(jnp.inf)) for _ in seqs))


def _select(nkt, k_sel, q_start, seqs, stats):
    kf = float(k_sel)

    @pl.when(nkt % 2 == 1)
    def _():
        for sq in seqs:
            sq.idx[_key_rows(nkt), :] = jnp.full((KT, TQ), -jnp.inf, F32)
    npairs = (nkt + 1) // 2

    def count_ge(ts):
        tbs = [jnp.broadcast_to(t, (SUBLANES, TQ)) for t in ts]

        def pair(i, accs):
            out = []
            for sq, tb, acc in zip(seqs, tbs, accs):
                tiles = sq.idx.at[pl.ds(pl.multiple_of(i * (2 * KT), 2 * KT), 2 * KT), :]
                chains = [None] * COUNT_CHAINS
                for r in range(2 * KT // SUBLANES):
                    hit = jnp.where(tiles[r * SUBLANES:(r + 1) * SUBLANES, :] >= tb, 1.0, 0.0)
                    c = r % COUNT_CHAINS
                    chains[c] = hit if chains[c] is None else chains[c] + hit
                while len(chains) > 1:
                    chains = [chains[a] + chains[a + 1] for a in range(0, len(chains), 2)]
                out.append(acc + chains[0])
            return tuple(out)

        accs = lax.fori_loop(0, npairs, pair, tuple(jnp.zeros((SUBLANES, TQ), F32) for _ in seqs))
        return [jnp.sum(acc, axis=0, keepdims=True) for acc in accs]

    def step(states):
        mids = [lo + (hi - lo) * 0.5 for lo, hi, _ in states]
        out = []
        for (lo, hi, cnt_lo), mid, cnt in zip(states, mids, count_ge(mids)):
            up = cnt >= kf
            out.append((jnp.where(up, mid, lo), jnp.where(up, hi, mid), jnp.where(up, cnt, cnt_lo)))
        return out

    def unsettled(cnt_lo):
        return jnp.max(jnp.where(cnt_lo != kf, 1.0, 0.0))

    n = len(seqs)

    def cond(c):
        pending = c[1]
        for f in c[2:n + 1]:
            pending = jnp.maximum(pending, f)
        return (c[0] < SEARCH_ROUNDS) & (pending > 0.0)

    def body(c):
        states = [c[n + 1 + 3 * s:n + 4 + 3 * s] for s in range(n)]
        for _ in range(SEARCH_UNROLL):
            states = step(states)
        return (c[0] + 1,) + tuple(unsettled(st[2]) for st in states) + tuple(x for st in states for x in st)

    n_causal = (lax.broadcasted_iota(I32, (1, TQ), 1) + (q_start + 1)).astype(F32)
    init = [(cmin, cmax, n_causal) for cmax, cmin in stats]
    out = lax.while_loop(cond, body, (jnp.int32(0),) + tuple(unsettled(st[2]) for st in init)
                         + tuple(x for st in init for x in st))
    for s, sq in enumerate(seqs):
        sq.lo[...] = out[n + 1 + 3 * s]
    return out[1:n + 1]


def _select_exact(nkt, k_sel, sq):
    def over_tiles(fn, init):
        return lax.fori_loop(0, nkt, lambda kt, c: fn(sq.idx[_key_rows(kt), :], c), init)

    def count(pred):
        return over_tiles(
            lambda x, c: c + jnp.sum(pred(_sortable(x)).astype(I32), axis=0, keepdims=True),
            jnp.zeros((1, TQ), I32))

    def search(i, prefix):
        cand = prefix | lax.shift_left(jnp.int32(1), 31 - i)
        return jnp.where(count(lambda key: key >= (cand ^ INT_MIN)) >= k_sel, cand, prefix)

    thr = lax.fori_loop(0, 32, search, jnp.zeros((1, TQ), I32)) ^ INT_MIN
    need = (k_sel - count(lambda key: key > thr)).astype(F32)
    r = lax.broadcasted_iota(I32, (KT, KT), 0)
    c = lax.broadcasted_iota(I32, (KT, KT), 1)
    earlier = (c < r).astype(BF16)

    def mark(kt, carry):
        x = sq.idx[_key_rows(kt), :]
        key = _sortable(x)
        eq = key == thr
        eqf = jnp.where(eq, 1.0, 0.0)
        rank = jnp.dot(earlier, eqf.astype(BF16), preferred_element_type=F32) + carry
        keep = ((key > thr) | (eq & (rank < need))) & (x > -jnp.inf)
        sq.idx[_key_rows(kt), :] = jnp.where(keep, 1.0, -jnp.inf)
        return carry + jnp.sum(eqf, axis=0, keepdims=True)

    lax.fori_loop(0, nkt, mark, jnp.zeros((1, TQ), F32))
    sq.lo[...] = jnp.full((1, TQ), 0.5, F32)


def _attend(nkt, seqs):
    for sq in seqs:
        for hh in range(ATTN_HEADS):
            g, jj = divmod(hh, GROUP)
            sq.qs[g, jj * TQ:(jj + 1) * TQ, :] = sq.q[:, hh * HEAD_DIM:(hh + 1) * HEAD_DIM]
        sq.m[...] = jnp.full(sq.m.shape, M_INIT, F32)
        sq.l[...] = jnp.zeros(sq.l.shape, F32)
        sq.o[...] = jnp.zeros(sq.o.shape, F32)
    los = [sq.lo[...] for sq in seqs]

    def key_tile(kt, carry):
        rows = _key_rows(kt)
        for sq, lo in zip(seqs, los):
            bias = jnp.where(sq.idx[rows, :] >= lo, 0.0, NEG_MASK)
            scores = [lax.dot_general(sq.k[rows, g * HEAD_DIM:(g + 1) * HEAD_DIM], sq.qs[g], _NT,
                                      preferred_element_type=F32) for g in range(KV_HEADS)]
            for g in range(KV_HEADS):
                probs, alphas = [], []
                for jj in range(GROUP):
                    hh = g * GROUP + jj
                    s = scores[g][:, jj * TQ:(jj + 1) * TQ] + bias
                    m_old = sq.m[hh]
                    m_new = jnp.maximum(m_old, jnp.max(s, axis=0, keepdims=True))
                    alphas.append(jnp.exp2(m_old - m_new))
                    probs.append(jnp.exp2(s - m_new).astype(BF16))
                    sq.m[hh] = m_new
                pv = jnp.dot(sq.vt[kt, g * VT_ROWS:(g + 1) * VT_ROWS, :], jnp.concatenate(probs, axis=1),
                             preferred_element_type=F32)
                for jj in range(GROUP):
                    hh = g * GROUP + jj
                    cols = slice(jj * TQ, (jj + 1) * TQ)
                    orow = slice(hh * HEAD_DIM, (hh + 1) * HEAD_DIM)
                    sq.o[orow, :] = sq.o[orow, :] * alphas[jj] + pv[0:HEAD_DIM, cols]
                    sq.l[hh] = sq.l[hh] * alphas[jj] + pv[HEAD_DIM:HEAD_DIM + 1, cols]
        return carry

    lax.fori_loop(0, nkt, key_tile, 0)
    for sq in seqs:
        for hh in range(ATTN_HEADS):
            orow = slice(hh * HEAD_DIM, (hh + 1) * HEAD_DIM)
            sq.o[orow, :] = sq.o[orow, :] / sq.l[hh]
        sq.y[...] = sq.o[...].T.astype(BF16)


def _attn_kernel(seq, k_sel, *refs):
    seqs = [_Seq(*(r.at[s] for r in refs)) for s in range(SEQ_TILE)]
    j = pl.program_id(1)
    nkt = j + 1
    q_start = j * TQ

    @pl.when(j == 0)
    def _():
        for sq in seqs:
            for kt in range(seq // KT):
                v_t = sq.v[kt * KT:(kt + 1) * KT, :].astype(F32).T.astype(BF16)
                for g in range(KV_HEADS):
                    sq.vt[kt, g * VT_ROWS:g * VT_ROWS + HEAD_DIM, :] = v_t[g * HEAD_DIM:(g + 1) * HEAD_DIM, :]
                    sq.vt[kt, g * VT_ROWS + HEAD_DIM:(g + 1) * VT_ROWS, :] = jnp.ones(
                        (VT_ROWS - HEAD_DIM, KT), BF16)

    all_selected = (j + 1) * TQ <= k_sel

    @pl.when(all_selected)
    def _():
        def fill(kt, carry):
            kpos = lax.broadcasted_iota(I32, (KT, TQ), 0) + kt * KT
            qpos = lax.broadcasted_iota(I32, (KT, TQ), 1) + q_start
            for sq in seqs:
                sq.idx[_key_rows(kt), :] = jnp.where(kpos <= qpos, 1.0, -jnp.inf)
            return carry
        lax.fori_loop(0, nkt, fill, 0)
        for sq in seqs:
            sq.lo[...] = jnp.full((1, TQ), 0.5, F32)

    @pl.when(jnp.logical_not(all_selected))
    def _():
        stats = _indexer(nkt, q_start, seqs)
        failed = _select(nkt, k_sel, q_start, seqs, stats)
        for sq, f in zip(seqs, failed):
            @pl.when(f > 0.0)
            def _(sq=sq):
                _select_exact(nkt, k_sel, sq)

    _attend(nkt, seqs)


def _attention(qi, wi, q, ki, k, v, batch, seq):
    nq = seq // TQ
    k_sel = min(INDEX_TOPK, seq // 4)
    per_seq = lambda a: a.reshape(batch, seq, a.shape[-1])
    tile = lambda b, j: (b, j, 0)
    whole = lambda b, j: (b, 0, 0)
    scratch = lambda shape, dtype: pltpu.VMEM((SEQ_TILE,) + shape, dtype)
    y = pl.pallas_call(
        functools.partial(_attn_kernel, seq, k_sel),
        out_shape=jax.ShapeDtypeStruct((batch, seq, ATTN_WIDTH), BF16),
        grid=(batch // SEQ_TILE, nq),
        in_specs=[
            pl.BlockSpec((SEQ_TILE, TQ, IDX_HEADS * IDX_DIM), tile),
            pl.BlockSpec((SEQ_TILE, TQ, LANES), tile),
            pl.BlockSpec((SEQ_TILE, TQ, ATTN_WIDTH), tile),
            pl.BlockSpec((SEQ_TILE, seq, LANES), whole),
            pl.BlockSpec((SEQ_TILE, seq, LANES), whole),
            pl.BlockSpec((SEQ_TILE, seq, LANES), whole),
        ],
        out_specs=pl.BlockSpec((SEQ_TILE, TQ, ATTN_WIDTH), tile),
        scratch_shapes=[
            scratch((seq, TQ), F32),
            scratch((1, TQ), F32),
            scratch((IDX_HEADS, TQ, IDX_DIM), BF16),
            scratch((IDX_HEADS, TQ, LANES), F32),
            scratch((KV_HEADS, GROUP * TQ, HEAD_DIM), BF16),
            scratch((seq // KT, KV_HEADS * VT_ROWS, KT), BF16),
            scratch((ATTN_HEADS, 1, TQ), F32),
            scratch((ATTN_HEADS, 1, TQ), F32),
            scratch((ATTN_WIDTH, TQ), F32),
        ],
        compiler_params=pltpu.CompilerParams(
            dimension_semantics=("arbitrary", "arbitrary"), vmem_limit_bytes=VMEM_LIMIT_BYTES),
        name="dsa_attention",
    )(per_seq(qi), per_seq(wi), per_seq(q), per_seq(ki), per_seq(k), per_seq(v))
    return y.reshape(batch * seq, ATTN_WIDTH)


def _mix_kernel(y_ref, uc_ref, uch_ref, up_ref, uph_ref, h_ref, dw_ref, cb_ref, lg_ref, lb_ref,
                pw_ref, plw_ref, psc_ref, wo_ref, gpost_ref, o_ref, ypad_ref, upad_ref):
    j = pl.program_id(1)
    has_prev = (j > 0).astype(F32)

    def glu(u):
        return u[:, :CONV_WIDTH] * jax.nn.sigmoid(u[:, CONV_WIDTH:])

    def fill_shifted(ref, halo_rows, tile_rows):
        ref[0, 0:HALO, :] = halo_rows
        ref[0, HALO:HALO + TM, :] = tile_rows
        for s in range(1, SUBLANES):
            ref[s, 0:HALO + TM - SUBLANES, :] = ref[0, s:s + HALO + TM - SUBLANES, :]

    def rows_from(ref, off):
        s = off % SUBLANES
        return ref[s, off - s:off - s + TM, :]

    fill_shifted(ypad_ref, glu(uch_ref[...]) * has_prev, glu(uc_ref[...]))
    acc = None
    for tap in range(CONV_KERNEL):
        term = rows_from(ypad_ref, HALO - (CONV_KERNEL - 1) + tap) * dw_ref[tap:tap + 1, :]
        acc = term if acc is None else acc + term
    yc = acc + cb_ref[...]
    mu = jnp.mean(yc, axis=-1, keepdims=True)
    var = jnp.mean(jnp.square(yc - mu), axis=-1, keepdims=True)
    yn = (yc - mu) * lax.rsqrt(var + LN_EPS) * lg_ref[...] + lb_ref[...]
    ys = yn * jax.nn.sigmoid(yn)
    y_conv = jnp.dot(ys.astype(BF16), pw_ref[...], preferred_element_type=F32)

    u0 = up_ref[...]
    fill_shifted(upad_ref, uph_ref[...] * has_prev, u0)

    def back(i):
        return rows_from(upad_ref, HALO - i)

    sums = []
    run = u0
    nxt = 1
    for w in POOL_WINDOWS:
        while nxt < w:
            run = run + back(nxt)
            nxt += 1
        sums.append(run)
    lane = lax.broadcasted_iota(I32, (TM, POOL_WIDTH), 1)
    win_sum = sums[-1]
    win = jnp.full((TM, POOL_WIDTH), POOL_WINDOWS[-1], I32)
    for gi in range(POOL_GROUPS - 2, -1, -1):
        in_group = lane < (gi + 1) * POOL_GROUP_DIM
        win_sum = jnp.where(in_group, sums[gi], win_sum)
        win = jnp.where(in_group, POOL_WINDOWS[gi], win)
    t = lax.broadcasted_iota(I32, (TM, POOL_WIDTH), 0) + j * TM
    count = jnp.minimum(t + 1, win).astype(F32)
    pooled = win_sum / count - u0
    y_pool = jnp.dot(pooled.astype(BF16), plw_ref[...], preferred_element_type=F32) * psc_ref[...]

    c0, c1 = ATTN_WIDTH, ATTN_WIDTH + CONV_WIDTH
    mix = (jnp.dot(y_ref[...], wo_ref[0:c0, :], preferred_element_type=F32)
           + jnp.dot(y_conv.astype(BF16), wo_ref[c0:c1, :], preferred_element_type=F32)
           + jnp.dot(y_pool.astype(BF16), wo_ref[c1:, :], preferred_element_type=F32))
    o_ref[...] = h_ref[...] + _rms(mix, gpost_ref[...])


def _mix(layer, y, uc, up, h, dw, cb, lg, lb, pw, plw, psc, wo, gpost, batch, seq):
    T = batch * seq
    nt = seq // TM
    tile = lambda b, j: (b * nt + j, 0)
    halo_per_tile = TM // HALO
    halo = lambda b, j: (b * (seq // HALO) + jnp.maximum(j * halo_per_tile - 1, 0), 0)
    return pl.pallas_call(
        _mix_kernel,
        out_shape=jax.ShapeDtypeStruct((T, D_MODEL), F32),
        grid=(batch, nt),
        in_specs=[
            pl.BlockSpec((TM, ATTN_WIDTH), tile),
            pl.BlockSpec((TM, 2 * CONV_WIDTH), tile),
            pl.BlockSpec((HALO, 2 * CONV_WIDTH), halo),
            pl.BlockSpec((TM, POOL_WIDTH), tile),
            pl.BlockSpec((HALO, POOL_WIDTH), halo),
            pl.BlockSpec((TM, D_MODEL), tile),
            _const_spec((HALO, CONV_WIDTH)),
            _const_spec((1, CONV_WIDTH)),
            _const_spec((1, CONV_WIDTH)),
            _const_spec((1, CONV_WIDTH)),
            _layer_spec(layer, (CONV_WIDTH, CONV_WIDTH)),
            _layer_spec(layer, (POOL_WIDTH, POOL_WIDTH)),
            _const_spec((1, POOL_WIDTH)),
            _layer_spec(layer, (D_MODEL, D_MODEL)),
            _const_spec((1, D_MODEL)),
        ],
        out_specs=pl.BlockSpec((TM, D_MODEL), tile),
        scratch_shapes=[
            pltpu.VMEM((SUBLANES, HALO + TM, CONV_WIDTH), F32),
            pltpu.VMEM((SUBLANES, HALO + TM, POOL_WIDTH), F32),
        ],
        compiler_params=pltpu.CompilerParams(
            dimension_semantics=("parallel", "parallel"), vmem_limit_bytes=VMEM_LIMIT_BYTES),
        name="mix_out",
    )(y, uc, uc, up, up, h, dw, cb, lg, lb, pw, plw, psc, wo, gpost)


FF_CHUNK = 1024


def _mlp_kernel(h_ref, p_ref, gpre_ref, wup_ref, wdn_ref, gpost_ref, wg_ref, wp_ref, o_ref):
    h = h_ref[...]
    m = _rms(h, gpre_ref[...]).astype(BF16)
    acc = None
    for c in range(D_FF // FF_CHUNK):
        cs = slice(c * FF_CHUNK, (c + 1) * FF_CHUNK)
        r = jnp.maximum(jnp.dot(m, wup_ref[:, cs], preferred_element_type=F32), 0.0)
        d = jnp.dot((r * r).astype(BF16), wdn_ref[cs, :], preferred_element_type=F32)
        acc = d if acc is None else acc + d
    h2 = h + _rms(acc, gpost_ref[...])
    gate = jax.nn.sigmoid(jnp.dot(h2.astype(BF16), wg_ref[...], preferred_element_type=F32))
    emb = jnp.dot(p_ref[...].astype(BF16), wp_ref[...], preferred_element_type=F32)
    o_ref[...] = h2 + gate * emb


def _mlp(layer, h, p, gpre, wup, wdn, gpost, wg, wp):
    T = h.shape[0]
    row = lambda i: (i, 0)
    return pl.pallas_call(
        _mlp_kernel,
        out_shape=jax.ShapeDtypeStruct((T, D_MODEL), F32),
        grid=(T // TM,),
        in_specs=[
            pl.BlockSpec((TM, D_MODEL), row),
            pl.BlockSpec((None, TM, PLE_DIM), lambda i: (layer, i, 0)),
            _const_spec((1, D_MODEL)),
            _layer_spec(layer, (D_MODEL, D_FF)),
            _layer_spec(layer, (D_FF, D_MODEL)),
            _const_spec((1, D_MODEL)),
            _layer_spec(layer, (D_MODEL, D_MODEL)),
            _layer_spec(layer, (PLE_DIM, D_MODEL)),
        ],
        out_specs=pl.BlockSpec((TM, D_MODEL), row),
        compiler_params=pltpu.CompilerParams(
            dimension_semantics=("parallel",), vmem_limit_bytes=VMEM_LIMIT_BYTES),
        name="mlp_ple",
    )(h, p, gpre, wup, wdn, gpost, wg, wp)


def _lane_constants():
    inv_freq = ROPE_THETA ** (-jnp.arange(0, ROPE_DIM, 2, dtype=F32) / ROPE_DIM)
    lane = np.arange(LANES) % HEAD_DIM
    rot = lane < ROPE_DIM
    freq = jnp.where(jnp.asarray(rot), inv_freq[lane % ROPE_HALF], 0.0)
    lo = jnp.asarray(np.where(lane < ROPE_HALF, -1.0, 0.0), F32)
    hi = jnp.asarray(np.where(rot & (lane >= ROPE_HALF), 1.0, 0.0), F32)
    return jnp.concatenate([jnp.stack([freq, lo, hi]), jnp.zeros((5, LANES), F32)], axis=0)


def kernel(x, p, positions, norm_mix_pre, w_in, conv_dw, conv_b, conv_ln_g, conv_ln_b, conv_pw,
           pool_w, pool_scale, w_out, norm_mix_post, norm_mlp_pre, w_up, w_down, norm_mlp_post,
           ple_proj, ple_gate):
    batch, seq, d_model = x.shape
    depth = w_in.shape[0]
    assert d_model == D_MODEL and seq % TQ == 0 and seq % TM == 0 and TQ == KT and batch % SEQ_TILE == 0
    T = batch * seq

    w_in_pad = jnp.concatenate(
        [w_in[:, :, :IN_UNPADDED_SPLIT],
         jnp.zeros((depth, D_MODEL, COL_CONV - IN_UNPADDED_SPLIT), w_in.dtype),
         w_in[:, :, IN_UNPADDED_SPLIT:]], axis=2).astype(BF16)
    dw_pad = jnp.concatenate([conv_dw, jnp.zeros((depth, HALO - CONV_KERNEL, CONV_WIDTH), F32)], axis=1)
    pool_bd = jnp.einsum('lgcd,gh->lgchd', pool_w, jnp.eye(POOL_GROUPS, dtype=F32)).reshape(
        depth, POOL_WIDTH, POOL_WIDTH).astype(BF16)
    conv_pw_b = conv_pw.astype(BF16)
    w_out_b = w_out.astype(BF16)
    w_up_b = w_up.astype(BF16)
    w_down_b = w_down.astype(BF16)
    gate_b = ple_gate.astype(BF16)
    proj_b = ple_proj.astype(BF16)
    vec = lambda a, i: a[i].reshape(1, -1)

    lane_consts = _lane_constants()
    pos = positions.reshape(T, 1)
    h = x.reshape(T, D_MODEL)
    p2 = p.reshape(depth, T, PLE_DIM)
    for i in range(depth):
        q, k, v, qi, ki, wi, uc, up = _inproj(i, h, pos, vec(norm_mix_pre, i), w_in_pad, lane_consts)
        y = _attention(qi, wi, q, ki, k, v, batch, seq)
        h = _mix(i, y, uc, up, h, dw_pad[i], vec(conv_b, i), vec(conv_ln_g, i), vec(conv_ln_b, i),
                 conv_pw_b, pool_bd, vec(pool_scale, i), w_out_b, vec(norm_mix_post, i), batch, seq)
        h = _mlp(i, h, p2, vec(norm_mlp_pre, i), w_up_b, w_down_b, vec(norm_mlp_post, i),
                 gate_b, proj_b)
    return h.reshape(batch, seq, D_MODEL)
```

```python
import collections
import functools

import numpy as np
import jax
import jax.numpy as jnp
from jax import lax
from jax.experimental import pallas as pl
from jax.experimental.pallas import tpu as pltpu

F32 = jnp.float32
BF16 = jnp.bfloat16
I32 = jnp.int32

D_MODEL = 1024
HEAD_DIM = 64
ATTN_WIDTH = 512
ATTN_HEADS = 8
KV_HEADS = 2
GROUP = ATTN_HEADS // KV_HEADS
IDX_HEADS = 16
IDX_DIM = 64
INDEX_TOPK = 256
CONV_WIDTH = 256
CONV_KERNEL = 31
POOL_WIDTH = 256
POOL_GROUPS = 4
POOL_GROUP_DIM = 64
POOL_WINDOWS = (2, 4, 8, 16)
D_FF = 4096
PLE_DIM = 256
ROPE_THETA = 500000.0
ROPE_DIM = 16
ROPE_HALF = ROPE_DIM // 2
NORM_EPS = 1e-6
LN_EPS = 1e-5
ATT_SCALE = HEAD_DIM ** -0.5
Q_SCALE = ATT_SCALE * float(np.log2(np.e))
IDX_SCALE = (IDX_DIM ** -0.5) * (IDX_HEADS ** -0.5)

LANES = 128
SUBLANES = 8
VMEM_LIMIT_BYTES = 56 * 1024 * 1024

COL_Q = 0
COL_K = COL_Q + ATTN_WIDTH
COL_V = COL_K + KV_HEADS * HEAD_DIM
COL_QI = COL_V + KV_HEADS * HEAD_DIM
COL_KIWI = COL_QI + IDX_HEADS * IDX_DIM
COL_CONV = COL_KIWI + LANES
COL_POOL = COL_CONV + 2 * CONV_WIDTH
IN_PAD_WIDTH = COL_POOL + POOL_WIDTH
IN_UNPADDED_SPLIT = ATTN_WIDTH + 2 * KV_HEADS * HEAD_DIM + IDX_HEADS * IDX_DIM + IDX_DIM + IDX_HEADS
WI_LANE0 = IDX_DIM

TM = 512
TQ = 256
KT = 256
QH = 128
SEQ_TILE = 4
HALO = 32
SEARCH_UNROLL = 4
COUNT_CHAINS = 8
SEARCH_ROUNDS = 12
VT_ROWS = HEAD_DIM + 16

NEG_MASK = -1e30
M_INIT = -2e30
INT_MIN = -2 ** 31


def _const_spec(shape):
    zeros = (0,) * len(shape)
    return pl.BlockSpec(shape, lambda *_: zeros, pipeline_mode=pl.Buffered(1))


def _layer_spec(layer, shape):
    index = (layer,) + (0,) * len(shape)
    return pl.BlockSpec((None,) + tuple(shape), lambda *_: index, pipeline_mode=pl.Buffered(1))


def _rms(x, gain):
    return x * lax.rsqrt(jnp.mean(x * x, axis=-1, keepdims=True) + NORM_EPS) * gain


def _inproj_kernel(h_ref, pos_ref, g_ref, w_ref, lc_ref,
                   q_ref, k_ref, v_ref, qi_ref, ki_ref, wi_ref, uc_ref, up_ref):
    a = _rms(h_ref[...], g_ref[...]).astype(BF16)
    pos = pos_ref[...].astype(F32)
    ang = pos * lc_ref[0:1, :]
    cos = jnp.cos(ang)
    sin = jnp.sin(ang)
    s_lo = sin * lc_ref[1:2, :]
    s_hi = sin * lc_ref[2:3, :]

    def rope(x):
        return x * cos + pltpu.roll(x, LANES - ROPE_HALF, 1) * s_lo + pltpu.roll(x, ROPE_HALF, 1) * s_hi

    def proj(c0, c1):
        return jnp.dot(a, w_ref[:, c0:c1], preferred_element_type=F32)

    u = proj(COL_Q, COL_K)
    for s in range(ATTN_WIDTH // LANES):
        sl = slice(s * LANES, (s + 1) * LANES)
        q_ref[:, sl] = (rope(u[:, sl]) * Q_SCALE).astype(BF16)
    k_ref[...] = rope(proj(COL_K, COL_V)).astype(BF16)
    v_ref[...] = proj(COL_V, COL_QI).astype(BF16)
    u = proj(COL_QI, COL_KIWI)
    for s in range(IDX_HEADS * IDX_DIM // LANES):
        sl = slice(s * LANES, (s + 1) * LANES)
        qi_ref[:, sl] = rope(u[:, sl]).astype(BF16)
    u = proj(COL_KIWI, COL_CONV)
    ki_ref[...] = rope(u).astype(BF16)
    wi_ref[...] = u * IDX_SCALE
    uc_ref[...] = proj(COL_CONV, COL_POOL)
    up_ref[...] = proj(COL_POOL, IN_PAD_WIDTH)


def _inproj(layer, h, pos, gain, w, lane_consts):
    T = h.shape[0]
    row = lambda i: (i, 0)
    out_shape = (
        jax.ShapeDtypeStruct((T, ATTN_WIDTH), BF16),
        jax.ShapeDtypeStruct((T, LANES), BF16),
        jax.ShapeDtypeStruct((T, LANES), BF16),
        jax.ShapeDtypeStruct((T, IDX_HEADS * IDX_DIM), BF16),
        jax.ShapeDtypeStruct((T, LANES), BF16),
        jax.ShapeDtypeStruct((T, LANES), F32),
        jax.ShapeDtypeStruct((T, 2 * CONV_WIDTH), F32),
        jax.ShapeDtypeStruct((T, POOL_WIDTH), F32),
    )
    return pl.pallas_call(
        _inproj_kernel,
        out_shape=out_shape,
        grid=(T // TM,),
        in_specs=[
            pl.BlockSpec((TM, D_MODEL), row),
            pl.BlockSpec((TM, 1), row),
            _const_spec((1, D_MODEL)),
            _layer_spec(layer, (D_MODEL, IN_PAD_WIDTH)),
            _const_spec((8, LANES)),
        ],
        out_specs=tuple(pl.BlockSpec((TM, s.shape[1]), row) for s in out_shape),
        compiler_params=pltpu.CompilerParams(
            dimension_semantics=("parallel",), vmem_limit_bytes=VMEM_LIMIT_BYTES),
        name="inproj",
    )(h, pos, gain, w, lane_consts)


_NT = (((1,), (1,)), ((), ()))

_Seq = collections.namedtuple(
    "_Seq", "qi wi q ki k v y idx lo qis wb qs vt m l o")


def _key_rows(kt):
    return pl.ds(pl.multiple_of(kt * KT, KT), KT)


def _sortable(x):
    bits = pltpu.bitcast(x, I32)
    return bits ^ ((bits >> 31) & 0x7FFFFFFF)


def _indexer(nkt, q_start, seqs):
    for sq in seqs:
        for hh in range(IDX_HEADS):
            sq.qis[hh] = sq.qi[:, hh * IDX_DIM:(hh + 1) * IDX_DIM]
            sq.wb[hh] = jnp.broadcast_to(sq.wi[:, WI_LANE0 + hh:WI_LANE0 + hh + 1], (TQ, LANES))

    def key_tile(kt, carry):
        out = []
        for sq, (cmax, cmin) in zip(seqs, carry):
            ki = sq.ki[_key_rows(kt), 0:IDX_DIM]
            tmax, tmin = [], []
            for qh in range(TQ // QH):
                qs = slice(qh * QH, (qh + 1) * QH)
                acc = [None] * (KT // LANES)
                for hh in range(IDX_HEADS):
                    d = lax.dot_general(sq.qis[hh, qs, :], ki, _NT, preferred_element_type=F32)
                    w = sq.wb[hh, qs, :]
                    for c in range(KT // LANES):
                        term = jnp.maximum(d[:, c * LANES:(c + 1) * LANES], 0.0) * w
                        acc[c] = term if acc[c] is None else acc[c] + term
                qpos = lax.broadcasted_iota(I32, (QH, LANES), 0) + (q_start + qh * QH)
                hmax, hmin = None, None
                for c in range(KT // LANES):
                    kpos = lax.broadcasted_iota(I32, (QH, LANES), 1) + (kt * KT + c * LANES)
                    blk = jnp.where(kpos <= qpos, acc[c], -jnp.inf).T
                    rows = pl.ds(pl.multiple_of(kt * KT + c * LANES, LANES), LANES)
                    sq.idx[rows, qs] = blk
                    bmax = jnp.max(blk, axis=0, keepdims=True)
                    bmin = jnp.min(jnp.where(blk > -jnp.inf, blk, jnp.inf), axis=0, keepdims=True)
                    hmax = bmax if hmax is None else jnp.maximum(hmax, bmax)
                    hmin = bmin if hmin is None else jnp.minimum(hmin, bmin)
                tmax.append(hmax)
                tmin.append(hmin)
            out.append((jnp.maximum(cmax, jnp.concatenate(tmax, axis=1)),
                        jnp.minimum(cmin, jnp.concatenate(tmin, axis=1))))
        return tuple(out)

    row = lambda v: jnp.full((1, TQ), v, F32)
    return lax.fori_loop(0, nkt, key_tile, tuple((row(-jnp.inf), row(jnp.inf)) for _ in seqs))


def _select(nkt, k_sel, q_start, seqs, stats):
    kf = float(k_sel)

    @pl.when(nkt % 2 == 1)
    def _():
        for sq in seqs:
            sq.idx[_key_rows(nkt), :] = jnp.full((KT, TQ), -jnp.inf, F32)
    npairs = (nkt + 1) // 2

    def count_ge(ts):
        tbs = [jnp.broadcast_to(t, (SUBLANES, TQ)) for t in ts]

        def pair(i, accs):
            out = []
            for sq, tb, acc in zip(seqs, tbs, accs):
                tiles = sq.idx.at[pl.ds(pl.multiple_of(i * (2 * KT), 2 * KT), 2 * KT), :]
                chains = [None] * COUNT_CHAINS
                for r in range(2 * KT // SUBLANES):
                    hit = jnp.where(tiles[r * SUBLANES:(r + 1) * SUBLANES, :] >= tb, 1.0, 0.0)
                    c = r % COUNT_CHAINS
                    chains[c] = hit if chains[c] is None else chains[c] + hit
                while len(chains) > 1:
                    chains = [chains[a] + chains[a + 1] for a in range(0, len(chains), 2)]
                out.append(acc + chains[0])
            return tuple(out)

        accs = lax.fori_loop(0, npairs, pair, tuple(jnp.zeros((SUBLANES, TQ), F32) for _ in seqs))
        return [jnp.sum(acc, axis=0, keepdims=True) for acc in accs]

    def step(states):
        mids = [lo + (hi - lo) * 0.5 for lo, hi, _ in states]
        out = []
        for (lo, hi, cnt_lo), mid, cnt in zip(states, mids, count_ge(mids)):
            up = cnt >= kf
            out.append((jnp.where(up, mid, lo), jnp.where(up, hi, mid), jnp.where(up, cnt, cnt_lo)))
        return out

    def unsettled(cnt_lo):
        return jnp.max(jnp.where(cnt_lo != kf, 1.0, 0.0))

    n = len(seqs)

    def cond(c):
        pending = c[1]
        for f in c[2:n + 1]:
            pending = jnp.maximum(pending, f)
        return (c[0] < SEARCH_ROUNDS) & (pending > 0.0)

    def body(c):
        states = [c[n + 1 + 3 * s:n + 4 + 3 * s] for s in range(n)]
        for _ in range(SEARCH_UNROLL):
            states = step(states)
        return (c[0] + 1,) + tuple(unsettled(st[2]) for st in states) + tuple(x for st in states for x in st)

    n_causal = (lax.broadcasted_iota(I32, (1, TQ), 1) + (q_start + 1)).astype(F32)
    init = [(cmin, cmax, n_causal) for cmax, cmin in stats]
    out = lax.while_loop(cond, body, (jnp.int32(0),) + tuple(unsettled(st[2]) for st in init)
                         + tuple(x for st in init for x in st))
    for s, sq in enumerate(seqs):
        sq.lo[...] = out[n + 1 + 3 * s]
    return out[1:n + 1]


def _select_exact(nkt, k_sel, sq):
    def over_tiles(fn, init):
        return lax.fori_loop(0, nkt, lambda kt, c: fn(sq.idx[_key_rows(kt), :], c), init)

    def count(pred):
        return over_tiles(
            lambda x, c: c + jnp.sum(pred(_sortable(x)).astype(I32), axis=0, keepdims=True),
            jnp.zeros((1, TQ), I32))

    def search(i, prefix):
        cand = prefix | lax.shift_left(jnp.int32(1), 31 - i)
        return jnp.where(count(lambda key: key >= (cand ^ INT_MIN)) >= k_sel, cand, prefix)

    thr = lax.fori_loop(0, 32, search, jnp.zeros((1, TQ), I32)) ^ INT_MIN
    need = (k_sel - count(lambda key: key > thr)).astype(F32)
    r = lax.broadcasted_iota(I32, (KT, KT), 0)
    c = lax.broadcasted_iota(I32, (KT, KT), 1)
    earlier = (c < r).astype(BF16)

    def mark(kt, carry):
        x = sq.idx[_key_rows(kt), :]
        key = _sortable(x)
        eq = key == thr
        eqf = jnp.where(eq, 1.0, 0.0)
        rank = jnp.dot(earlier, eqf.astype(BF16), preferred_element_type=F32) + carry
        keep = ((key > thr) | (eq & (rank < need))) & (x > -jnp.inf)
        sq.idx[_key_rows(kt), :] = jnp.where(keep, 1.0, -jnp.inf)
        return carry + jnp.sum(eqf, axis=0, keepdims=True)

    lax.fori_loop(0, nkt, mark, jnp.zeros((1, TQ), F32))
    sq.lo[...] = jnp.full((1, TQ), 0.5, F32)


def _attend(nkt, seqs):
    for sq in seqs:
        for hh in range(ATTN_HEADS):
            g, jj = divmod(hh, GROUP)
            sq.qs[g, jj * TQ:(jj + 1) * TQ, :] = sq.q[:, hh * HEAD_DIM:(hh + 1) * HEAD_DIM]
        sq.m[...] = jnp.full(sq.m.shape, M_INIT, F32)
        sq.l[...] = jnp.zeros(sq.l.shape, F32)
        sq.o[...] = jnp.zeros(sq.o.shape, F32)
    los = [sq.lo[...] for sq in seqs]

    def key_tile(kt, carry):
        rows = _key_rows(kt)
        for sq, lo in zip(seqs, los):
            bias = jnp.where(sq.idx[rows, :] >= lo, 0.0, NEG_MASK)
            scores = [lax.dot_general(sq.k[rows, g * HEAD_DIM:(g + 1) * HEAD_DIM], sq.qs[g], _NT,
                                      preferred_element_type=F32) for g in range(KV_HEADS)]
            for g in range(KV_HEADS):
                probs, alphas = [], []
                for jj in range(GROUP):
                    hh = g * GROUP + jj
                    s = scores[g][:, jj * TQ:(jj + 1) * TQ] + bias
                    m_old = sq.m[hh]
                    m_new = jnp.maximum(m_old, jnp.max(s, axis=0, keepdims=True))
                    alphas.append(jnp.exp2(m_old - m_new))
                    probs.append(jnp.exp2(s - m_new).astype(BF16))
                    sq.m[hh] = m_new
                pv = jnp.dot(sq.vt[kt, g * VT_ROWS:(g + 1) * VT_ROWS, :], jnp.concatenate(probs, axis=1),
                             preferred_element_type=F32)
                for jj in range(GROUP):
                    hh = g * GROUP + jj
                    cols = slice(jj * TQ, (jj + 1) * TQ)
                    orow = slice(hh * HEAD_DIM, (hh + 1) * HEAD_DIM)
                    sq.o[orow, :] = sq.o[orow, :] * alphas[jj] + pv[0:HEAD_DIM, cols]
                    sq.l[hh] = sq.l[hh] * alphas[jj] + pv[HEAD_DIM:HEAD_DIM + 1, cols]
        return carry

    lax.fori_loop(0, nkt, key_tile, 0)
    for sq in seqs:
        for hh in range(ATTN_HEADS):
            orow = slice(hh * HEAD_DIM, (hh + 1) * HEAD_DIM)
            sq.o[orow, :] = sq.o[orow, :] / sq.l[hh]
        sq.y[...] = sq.o[...].T.astype(BF16)


def _attn_kernel(seq, k_sel, *refs):
    seqs = [_Seq(*(r.at[s] for r in refs)) for s in range(SEQ_TILE)]
    j = pl.program_id(1)
    nkt = j + 1
    q_start = j * TQ

    @pl.when(j == 0)
    def _():
        for sq in seqs:
            for kt in range(seq // KT):
                v_t = sq.v[kt * KT:(kt + 1) * KT, :].astype(F32).T.astype(BF16)
                for g in range(KV_HEADS):
                    sq.vt[kt, g * VT_ROWS:g * VT_ROWS + HEAD_DIM, :] = v_t[g * HEAD_DIM:(g + 1) * HEAD_DIM, :]
                    sq.vt[kt, g * VT_ROWS + HEAD_DIM:(g + 1) * VT_ROWS, :] = jnp.ones(
                        (VT_ROWS - HEAD_DIM, KT), BF16)

    all_selected = (j + 1) * TQ <= k_sel

    @pl.when(all_selected)
    def _():
        def fill(kt, carry):
            kpos = lax.broadcasted_iota(I32, (KT, TQ), 0) + kt * KT
            qpos = lax.broadcasted_iota(I32, (KT, TQ), 1) + q_start
            for sq in seqs:
                sq.idx[_key_rows(kt), :] = jnp.where(kpos <= qpos, 1.0, -jnp.inf)
            return carry
        lax.fori_loop(0, nkt, fill, 0)
        for sq in seqs:
            sq.lo[...] = jnp.full((1, TQ), 0.5, F32)

    @pl.when(jnp.logical_not(all_selected))
    def _():
        stats = _indexer(nkt, q_start, seqs)
        failed = _select(nkt, k_sel, q_start, seqs, stats)
        for sq, f in zip(seqs, failed):
            @pl.when(f > 0.0)
            def _(sq=sq):
                _select_exact(nkt, k_sel, sq)

    _attend(nkt, seqs)


def _attention(qi, wi, q, ki, k, v, batch, seq):
    nq = seq // TQ
    k_sel = min(INDEX_TOPK, seq // 4)
    per_seq = lambda a: a.reshape(batch, seq, a.shape[-1])
    tile = lambda b, j: (b, j, 0)
    whole = lambda b, j: (b, 0, 0)
    scratch = lambda shape, dtype: pltpu.VMEM((SEQ_TILE,) + shape, dtype)
    y = pl.pallas_call(
        functools.partial(_attn_kernel, seq, k_sel),
        out_shape=jax.ShapeDtypeStruct((batch, seq, ATTN_WIDTH), BF16),
        grid=(batch // SEQ_TILE, nq),
        in_specs=[
            pl.BlockSpec((SEQ_TILE, TQ, IDX_HEADS * IDX_DIM), tile),
            pl.BlockSpec((SEQ_TILE, TQ, LANES), tile),
            pl.BlockSpec((SEQ_TILE, TQ, ATTN_WIDTH), tile),
            pl.BlockSpec((SEQ_TILE, seq, LANES), whole),
            pl.BlockSpec((SEQ_TILE, seq, LANES), whole),
            pl.BlockSpec((SEQ_TILE, seq, LANES), whole),
        ],
        out_specs=pl.BlockSpec((SEQ_TILE, TQ, ATTN_WIDTH), tile),
        scratch_shapes=[
            scratch((seq, TQ), F32),
            scratch((1, TQ), F32),
            scratch((IDX_HEADS, TQ, IDX_DIM), BF16),
            scratch((IDX_HEADS, TQ, LANES), F32),
            scratch((KV_HEADS, GROUP * TQ, HEAD_DIM), BF16),
            scratch((seq // KT, KV_HEADS * VT_ROWS, KT), BF16),
            scratch((ATTN_HEADS, 1, TQ), F32),
            scratch((ATTN_HEADS, 1, TQ), F32),
            scratch((ATTN_WIDTH, TQ), F32),
        ],
        compiler_params=pltpu.CompilerParams(
            dimension_semantics=("arbitrary", "arbitrary"), vmem_limit_bytes=VMEM_LIMIT_BYTES),
        name="dsa_attention",
    )(per_seq(qi), per_seq(wi), per_seq(q), per_seq(ki), per_seq(k), per_seq(v))
    return y.reshape(batch * seq, ATTN_WIDTH)


def _mix_kernel(y_ref, uc_ref, uch_ref, up_ref, uph_ref, h_ref, dw_ref, cb_ref, lg_ref, lb_ref,
                pw_ref, plw_ref, psc_ref, wo_ref, gpost_ref, o_ref, ypad_ref, upad_ref):
    j = pl.program_id(1)
    has_prev = (j > 0).astype(F32)

    def glu(u):
        return u[:, :CONV_WIDTH] * jax.nn.sigmoid(u[:, CONV_WIDTH:])

    def fill_shifted(ref, halo_rows, tile_rows):
        ref[0, 0:HALO, :] = halo_rows
        ref[0, HALO:HALO + TM, :] = tile_rows
        for s in range(1, SUBLANES):
            ref[s, 0:HALO + TM - SUBLANES, :] = ref[0, s:s + HALO + TM - SUBLANES, :]

    def rows_from(ref, off):
        s = off % SUBLANES
        return ref[s, off - s:off - s + TM, :]

    fill_shifted(ypad_ref, glu(uch_ref[...]) * has_prev, glu(uc_ref[...]))
    acc = None
    for tap in range(CONV_KERNEL):
        term = rows_from(ypad_ref, HALO - (CONV_KERNEL - 1) + tap) * dw_ref[tap:tap + 1, :]
        acc = term if acc is None else acc + term
    yc = acc + cb_ref[...]
    mu = jnp.mean(yc, axis=-1, keepdims=True)
    var = jnp.mean(jnp.square(yc - mu), axis=-1, keepdims=True)
    yn = (yc - mu) * lax.rsqrt(var + LN_EPS) * lg_ref[...] + lb_ref[...]
    ys = yn * jax.nn.sigmoid(yn)
    y_conv = jnp.dot(ys.astype(BF16), pw_ref[...], preferred_element_type=F32)

    u0 = up_ref[...]
    fill_shifted(upad_ref, uph_ref[...] * has_prev, u0)

    def back(i):
        return rows_from(upad_ref, HALO - i)

    sums = []
    run = u0
    nxt = 1
    for w in POOL_WINDOWS:
        while nxt < w:
            run = run + back(nxt)
            nxt += 1
        sums.append(run)
    lane = lax.broadcasted_iota(I32, (TM, POOL_WIDTH), 1)
    win_sum = sums[-1]
    win = jnp.full((TM, POOL_WIDTH), POOL_WINDOWS[-1], I32)
    for gi in range(POOL_GROUPS - 2, -1, -1):
        in_group = lane < (gi + 1) * POOL_GROUP_DIM
        win_sum = jnp.where(in_group, sums[gi], win_sum)
        win = jnp.where(in_group, POOL_WINDOWS[gi], win)
    t = lax.broadcasted_iota(I32, (TM, POOL_WIDTH), 0) + j * TM
    count = jnp.minimum(t + 1, win).astype(F32)
    pooled = win_sum / count - u0
    y_pool = jnp.dot(pooled.astype(BF16), plw_ref[...], preferred_element_type=F32) * psc_ref[...]

    c0, c1 = ATTN_WIDTH, ATTN_WIDTH + CONV_WIDTH
    mix = (jnp.dot(y_ref[...], wo_ref[0:c0, :], preferred_element_type=F32)
           + jnp.dot(y_conv.astype(BF16), wo_ref[c0:c1, :], preferred_element_type=F32)
           + jnp.dot(y_pool.astype(BF16), wo_ref[c1:, :], preferred_element_type=F32))
    o_ref[...] = h_ref[...] + _rms(mix, gpost_ref[...])


def _mix(layer, y, uc, up, h, dw, cb, lg, lb, pw, plw, psc, wo, gpost, batch, seq):
    T = batch * seq
    nt = seq // TM
    tile = lambda b, j: (b * nt + j, 0)
    halo_per_tile = TM // HALO
    halo = lambda b, j: (b * (seq // HALO) + jnp.maximum(j * halo_per_tile - 1, 0), 0)
    return pl.pallas_call(
        _mix_kernel,
        out_shape=jax.ShapeDtypeStruct((T, D_MODEL), F32),
        grid=(batch, nt),
        in_specs=[
            pl.BlockSpec((TM, ATTN_WIDTH), tile),
            pl.BlockSpec((TM, 2 * CONV_WIDTH), tile),
            pl.BlockSpec((HALO, 2 * CONV_WIDTH), halo),
            pl.BlockSpec((TM, POOL_WIDTH), tile),
            pl.BlockSpec((HALO, POOL_WIDTH), halo),
            pl.BlockSpec((TM, D_MODEL), tile),
            _const_spec((HALO, CONV_WIDTH)),
            _const_spec((1, CONV_WIDTH)),
            _const_spec((1, CONV_WIDTH)),
            _const_spec((1, CONV_WIDTH)),
            _layer_spec(layer, (CONV_WIDTH, CONV_WIDTH)),
            _layer_spec(layer, (POOL_WIDTH, POOL_WIDTH)),
            _const_spec((1, POOL_WIDTH)),
            _layer_spec(layer, (D_MODEL, D_MODEL)),
            _const_spec((1, D_MODEL)),
        ],
        out_specs=pl.BlockSpec((TM, D_MODEL), tile),
        scratch_shapes=[
            pltpu.VMEM((SUBLANES, HALO + TM, CONV_WIDTH), F32),
            pltpu.VMEM((SUBLANES, HALO + TM, POOL_WIDTH), F32),
        ],
        compiler_params=pltpu.CompilerParams(
            dimension_semantics=("parallel", "parallel"), vmem_limit_bytes=VMEM_LIMIT_BYTES),
        name="mix_out",
    )(y, uc, uc, up, up, h, dw, cb, lg, lb, pw, plw, psc, wo, gpost)


FF_CHUNK = 1024


def _mlp_kernel(h_ref, p_ref, gpre_ref, wup_ref, wdn_ref, gpost_ref, wg_ref, wp_ref, o_ref):
    h = h_ref[...]
    m = _rms(h, gpre_ref[...]).astype(BF16)
    acc = None
    for c in range(D_FF // FF_CHUNK):
        cs = slice(c * FF_CHUNK, (c + 1) * FF_CHUNK)
        r = jnp.maximum(jnp.dot(m, wup_ref[:, cs], preferred_element_type=F32), 0.0)
        d = jnp.dot((r * r).astype(BF16), wdn_ref[cs, :], preferred_element_type=F32)
        acc = d if acc is None else acc + d
    h2 = h + _rms(acc, gpost_ref[...])
    gate = jax.nn.sigmoid(jnp.dot(h2.astype(BF16), wg_ref[...], preferred_element_type=F32))
    emb = jnp.dot(p_ref[...].astype(BF16), wp_ref[...], preferred_element_type=F32)
    o_ref[...] = h2 + gate * emb


def _mlp(layer, h, p, gpre, wup, wdn, gpost, wg, wp):
    T = h.shape[0]
    row = lambda i: (i, 0)
    return pl.pallas_call(
        _mlp_kernel,
        out_shape=jax.ShapeDtypeStruct((T, D_MODEL), F32),
        grid=(T // TM,),
        in_specs=[
            pl.BlockSpec((TM, D_MODEL), row),
            pl.BlockSpec((None, TM, PLE_DIM), lambda i: (layer, i, 0)),
            _const_spec((1, D_MODEL)),
            _layer_spec(layer, (D_MODEL, D_FF)),
            _layer_spec(layer, (D_FF, D_MODEL)),
            _const_spec((1, D_MODEL)),
            _layer_spec(layer, (D_MODEL, D_MODEL)),
            _layer_spec(layer, (PLE_DIM, D_MODEL)),
        ],
        out_specs=pl.BlockSpec((TM, D_MODEL), row),
        compiler_params=pltpu.CompilerParams(
            dimension_semantics=("parallel",), vmem_limit_bytes=VMEM_LIMIT_BYTES),
        name="mlp_ple",
    )(h, p, gpre, wup, wdn, gpost, wg, wp)


def _lane_constants():
    inv_freq = ROPE_THETA ** (-jnp.arange(0, ROPE_DIM, 2, dtype=F32) / ROPE_DIM)
    lane = np.arange(LANES) % HEAD_DIM
    rot = lane < ROPE_DIM
    freq = jnp.where(jnp.asarray(rot), inv_freq[lane % ROPE_HALF], 0.0)
    lo = jnp.asarray(np.where(lane < ROPE_HALF, -1.0, 0.0), F32)
    hi = jnp.asarray(np.where(rot & (lane >= ROPE_HALF), 1.0, 0.0), F32)
    return jnp.concatenate([jnp.stack([freq, lo, hi]), jnp.zeros((5, LANES), F32)], axis=0)


def kernel(x, p, positions, norm_mix_pre, w_in, conv_dw, conv_b, conv_ln_g, conv_ln_b, conv_pw,
           pool_w, pool_scale, w_out, norm_mix_post, norm_mlp_pre, w_up, w_down, norm_mlp_post,
           ple_proj, ple_gate):
    batch, seq, d_model = x.shape
    depth = w_in.shape[0]
    assert d_model == D_MODEL and seq % TQ == 0 and seq % TM == 0 and TQ == KT and batch % SEQ_TILE == 0
    T = batch * seq

    w_in_pad = jnp.concatenate(
        [w_in[:, :, :IN_UNPADDED_SPLIT],
         jnp.zeros((depth, D_MODEL, COL_CONV - IN_UNPADDED_SPLIT), w_in.dtype),
         w_in[:, :, IN_UNPADDED_SPLIT:]], axis=2).astype(BF16)
    dw_pad = jnp.concatenate([conv_dw, jnp.zeros((depth, HALO - CONV_KERNEL, CONV_WIDTH), F32)], axis=1)
    pool_bd = jnp.einsum('lgcd,gh->lgchd', pool_w, jnp.eye(POOL_GROUPS, dtype=F32)).reshape(
        depth, POOL_WIDTH, POOL_WIDTH).astype(BF16)
    conv_pw_b = conv_pw.astype(BF16)
    w_out_b = w_out.astype(BF16)
    w_up_b = w_up.astype(BF16)
    w_down_b = w_down.astype(BF16)
    gate_b = ple_gate.astype(BF16)
    proj_b = ple_proj.astype(BF16)
    vec = lambda a, i: a[i].reshape(1, -1)

    lane_consts = _lane_constants()
    pos = positions.reshape(T, 1)
    h = x.reshape(T, D_MODEL)
    p2 = p.reshape(depth, T, PLE_DIM)
    for i in range(depth):
        q, k, v, qi, ki, wi, uc, up = _inproj(i, h, pos, vec(norm_mix_pre, i), w_in_pad, lane_consts)
        y = _attention(qi, wi, q, ki, k, v, batch, seq)
        h = _mix(i, y, uc, up, h, dw_pad[i], vec(conv_b, i), vec(conv_ln_g, i), vec(conv_ln_b, i),
                 conv_pw_b, pool_bd, vec(pool_scale, i), w_out_b, vec(norm_mix_post, i), batch, seq)
        h = _mlp(i, h, p2, vec(norm_mlp_pre, i), w_up_b, w_down_b, vec(norm_mlp_post, i),
                 gate_b, proj_b)
    return h.reshape(batch, seq, D_MODEL)
```

```python
import collections
import functools

import numpy as np
import jax
import jax.numpy as jnp
from jax import lax
from jax.experimental import pallas as pl
from jax.experimental.pallas import tpu as pltpu

F32 = jnp.float32
BF16 = jnp.bfloat16
I32 = jnp.int32

D_MODEL = 1024
HEAD_DIM = 64
ATTN_WIDTH = 512
ATTN_HEADS = 8
KV_HEADS = 2
GROUP = ATTN_HEADS // KV_HEADS
IDX_HEADS = 16
IDX_DIM = 64
INDEX_TOPK = 256
CONV_WIDTH = 256
CONV_KERNEL = 31
POOL_WIDTH = 256
POOL_GROUPS = 4
POOL_GROUP_DIM = 64
POOL_WINDOWS = (2, 4, 8, 16)
D_FF = 4096
PLE_DIM = 256
ROPE_THETA = 500000.0
ROPE_DIM = 16
ROPE_HALF = ROPE_DIM // 2
NORM_EPS = 1e-6
LN_EPS = 1e-5
ATT_SCALE = HEAD_DIM ** -0.5
Q_SCALE = ATT_SCALE * float(np.log2(np.e))
IDX_SCALE = (IDX_DIM ** -0.5) * (IDX_HEADS ** -0.5)

LANES = 128
SUBLANES = 8
VMEM_LIMIT_BYTES = 56 * 1024 * 1024

COL_Q = 0
COL_K = COL_Q + ATTN_WIDTH
COL_V = COL_K + KV_HEADS * HEAD_DIM
COL_QI = COL_V + KV_HEADS * HEAD_DIM
COL_KIWI = COL_QI + IDX_HEADS * IDX_DIM
COL_CONV = COL_KIWI + LANES
COL_POOL = COL_CONV + 2 * CONV_WIDTH
IN_PAD_WIDTH = COL_POOL + POOL_WIDTH
IN_UNPADDED_SPLIT = ATTN_WIDTH + 2 * KV_HEADS * HEAD_DIM + IDX_HEADS * IDX_DIM + IDX_DIM + IDX_HEADS
WI_LANE0 = IDX_DIM

TM = 512
TQ = 256
KT = 256
KH = 128
SEQ_TILE = 4
HALO = 32
SEARCH_UNROLL = 4
COUNT_CHAINS = 8
SEARCH_ROUNDS = 12
VT_ROWS = HEAD_DIM + 16

NEG_MASK = -1e30
M_INIT = -2e30
INT_MIN = -2 ** 31


def _const_spec(shape):
    zeros = (0,) * len(shape)
    return pl.BlockSpec(shape, lambda *_: zeros, pipeline_mode=pl.Buffered(1))


def _layer_spec(layer, shape):
    index = (layer,) + (0,) * len(shape)
    return pl.BlockSpec((None,) + tuple(shape), lambda *_: index, pipeline_mode=pl.Buffered(1))


def _rms(x, gain):
    return x * lax.rsqrt(jnp.mean(x * x, axis=-1, keepdims=True) + NORM_EPS) * gain


def _inproj_kernel(h_ref, pos_ref, g_ref, w_ref, lc_ref,
                   q_ref, k_ref, v_ref, qi_ref, ki_ref, wi_ref, uc_ref, up_ref):
    a = _rms(h_ref[...], g_ref[...]).astype(BF16)
    pos = pos_ref[...].astype(F32)
    ang = pos * lc_ref[0:1, :]
    cos = jnp.cos(ang)
    sin = jnp.sin(ang)
    s_lo = sin * lc_ref[1:2, :]
    s_hi = sin * lc_ref[2:3, :]

    def rope(x):
        return x * cos + pltpu.roll(x, LANES - ROPE_HALF, 1) * s_lo + pltpu.roll(x, ROPE_HALF, 1) * s_hi

    def proj(c0, c1):
        return jnp.dot(a, w_ref[:, c0:c1], preferred_element_type=F32)

    u = proj(COL_Q, COL_K)
    for s in range(ATTN_WIDTH // LANES):
        sl = slice(s * LANES, (s + 1) * LANES)
        q_ref[:, sl] = (rope(u[:, sl]) * Q_SCALE).astype(BF16)
    u = proj(COL_K, COL_QI)
    k_ref[...] = rope(u[:, :LANES]).astype(BF16)
    v_ref[...] = u[:, LANES:].astype(BF16)
    u = proj(COL_QI, COL_KIWI)
    for s in range(IDX_HEADS * IDX_DIM // LANES):
        sl = slice(s * LANES, (s + 1) * LANES)
        qi_ref[:, sl] = rope(u[:, sl]).astype(BF16)
    u = proj(COL_KIWI, COL_CONV)
    ki_ref[...] = rope(u).astype(BF16)
    wi_ref[...] = u * IDX_SCALE
    uc_ref[...] = proj(COL_CONV, COL_POOL)
    up_ref[...] = proj(COL_POOL, IN_PAD_WIDTH)


def _inproj(layer, h, pos, gain, w, lane_consts):
    T = h.shape[0]
    row = lambda i: (i, 0)
    out_shape = (
        jax.ShapeDtypeStruct((T, ATTN_WIDTH), BF16),
        jax.ShapeDtypeStruct((T, LANES), BF16),
        jax.ShapeDtypeStruct((T, LANES), BF16),
        jax.ShapeDtypeStruct((T, IDX_HEADS * IDX_DIM), BF16),
        jax.ShapeDtypeStruct((T, LANES), BF16),
        jax.ShapeDtypeStruct((T, LANES), F32),
        jax.ShapeDtypeStruct((T, 2 * CONV_WIDTH), F32),
        jax.ShapeDtypeStruct((T, POOL_WIDTH), F32),
    )
    return pl.pallas_call(
        _inproj_kernel,
        out_shape=out_shape,
        grid=(T // TM,),
        in_specs=[
            pl.BlockSpec((TM, D_MODEL), row),
            pl.BlockSpec((TM, 1), row),
            _const_spec((1, D_MODEL)),
            _layer_spec(layer, (D_MODEL, IN_PAD_WIDTH)),
            _const_spec((8, LANES)),
        ],
        out_specs=tuple(pl.BlockSpec((TM, s.shape[1]), row) for s in out_shape),
        compiler_params=pltpu.CompilerParams(
            dimension_semantics=("parallel",), vmem_limit_bytes=VMEM_LIMIT_BYTES),
        name="inproj",
    )(h, pos, gain, w, lane_consts)


_NT = (((1,), (1,)), ((), ()))

_Seq = collections.namedtuple(
    "_Seq", "qi wi q ki k v y idx lo qis qs vt m l o")


def _key_rows(kt):
    return pl.ds(pl.multiple_of(kt * KT, KT), KT)


def _sortable(x):
    bits = pltpu.bitcast(x, I32)
    return bits ^ ((bits >> 31) & 0x7FFFFFFF)


def _indexer(nkt, q_start, seqs):
    w_ts = []
    for sq in seqs:
        for hh in range(IDX_HEADS):
            sq.qis[hh] = sq.qi[:, hh * IDX_DIM:(hh + 1) * IDX_DIM]
        w_ts.append(sq.wi[...].T)

    def key_tile(kt, carry):
        out = []
        for sq, w_t, (cmax, cmin) in zip(seqs, w_ts, carry):
            for half in range(KT // KH):
                rows = pl.ds(pl.multiple_of(kt * KT + half * KH, KH), KH)
                ki = sq.ki[rows, 0:IDX_DIM]
                acc = None
                for hh in range(IDX_HEADS):
                    d = lax.dot_general(ki, sq.qis[hh], _NT, preferred_element_type=F32)
                    term = jnp.maximum(d, 0.0) * w_t[WI_LANE0 + hh:WI_LANE0 + hh + 1, :]
                    acc = term if acc is None else acc + term
                kpos = lax.broadcasted_iota(I32, (KH, TQ), 0) + (kt * KT + half * KH)
                qpos = lax.broadcasted_iota(I32, (KH, TQ), 1) + q_start
                blk = jnp.where(kpos <= qpos, acc, -jnp.inf)
                sq.idx[rows, :] = blk
                cmax = jnp.maximum(cmax, jnp.max(blk, axis=0, keepdims=True))
                cmin = jnp.minimum(
                    cmin, jnp.min(jnp.where(blk > -jnp.inf, blk, jnp.inf), axis=0, keepdims=True))
            out.append((cmax, cmin))
        return tuple(out)

    row = lambda v: jnp.full((1, TQ), v, F32)
    return lax.fori_loop(0, nkt, key_tile, tuple((row(-jnp.inf), row(jnp.inf)) for _ in seqs))


def _select(nkt, k_sel, q_start, seqs, stats):
    kf = float(k_sel)

    @pl.when(nkt % 2 == 1)
    def _():
        for sq in seqs:
            sq.idx[_key_rows(nkt), :] = jnp.full((KT, TQ), -jnp.inf, F32)
    npairs = (nkt + 1) // 2

    def count_ge(ts):
        tbs = [jnp.broadcast_to(t, (SUBLANES, TQ)) for t in ts]

        def pair(i, accs):
            out = []
            for sq, tb, acc in zip(seqs, tbs, accs):
                tiles = sq.idx.at[pl.ds(pl.multiple_of(i * (2 * KT), 2 * KT), 2 * KT), :]
                chains = [None] * COUNT_CHAINS
                for r in range(2 * KT // SUBLANES):
                    hit = jnp.where(tiles[r * SUBLANES:(r + 1) * SUBLANES, :] >= tb, 1.0, 0.0)
                    c = r % COUNT_CHAINS
                    chains[c] = hit if chains[c] is None else chains[c] + hit
                while len(chains) > 1:
                    chains = [chains[a] + chains[a + 1] for a in range(0, len(chains), 2)]
                out.append(acc + chains[0])
            return tuple(out)

        accs = lax.fori_loop(0, npairs, pair, tuple(jnp.zeros((SUBLANES, TQ), F32) for _ in seqs))
        return [jnp.sum(acc, axis=0, keepdims=True) for acc in accs]

    def step(states):
        mids = [lo + (hi - lo) * 0.5 for lo, hi, _ in states]
        out = []
        for (lo, hi, cnt_lo), mid, cnt in zip(states, mids, count_ge(mids)):
            up = cnt >= kf
            out.append((jnp.where(up, mid, lo), jnp.where(up, hi, mid), jnp.where(up, cnt, cnt_lo)))
        return out

    def unsettled(cnt_lo):
        return jnp.max(jnp.where(cnt_lo != kf, 1.0, 0.0))

    n = len(seqs)

    def cond(c):
        pending = c[1]
        for f in c[2:n + 1]:
            pending = jnp.maximum(pending, f)
        return (c[0] < SEARCH_ROUNDS) & (pending > 0.0)

    def body(c):
        states = [c[n + 1 + 3 * s:n + 4 + 3 * s] for s in range(n)]
        for _ in range(SEARCH_UNROLL):
            states = step(states)
        return (c[0] + 1,) + tuple(unsettled(st[2]) for st in states) + tuple(x for st in states for x in st)

    n_causal = (lax.broadcasted_iota(I32, (1, TQ), 1) + (q_start + 1)).astype(F32)
    init = [(cmin, cmax, n_causal) for cmax, cmin in stats]
    out = lax.while_loop(cond, body, (jnp.int32(0),) + tuple(unsettled(st[2]) for st in init)
                         + tuple(x for st in init for x in st))
    for s, sq in enumerate(seqs):
        sq.lo[...] = out[n + 1 + 3 * s]
    return out[1:n + 1]


def _select_exact(nkt, k_sel, sq):
    def over_tiles(fn, init):
        return lax.fori_loop(0, nkt, lambda kt, c: fn(sq.idx[_key_rows(kt), :], c), init)

    def count(pred):
        return over_tiles(
            lambda x, c: c + jnp.sum(pred(_sortable(x)).astype(I32), axis=0, keepdims=True),
            jnp.zeros((1, TQ), I32))

    def search(i, prefix):
        cand = prefix | lax.shift_left(jnp.int32(1), 31 - i)
        return jnp.where(count(lambda key: key >= (cand ^ INT_MIN)) >= k_sel, cand, prefix)

    thr = lax.fori_loop(0, 32, search, jnp.zeros((1, TQ), I32)) ^ INT_MIN
    need = (k_sel - count(lambda key: key > thr)).astype(F32)
    r = lax.broadcasted_iota(I32, (KT, KT), 0)
    c = lax.broadcasted_iota(I32, (KT, KT), 1)
    earlier = (c < r).astype(BF16)

    def mark(kt, carry):
        x = sq.idx[_key_rows(kt), :]
        key = _sortable(x)
        eq = key == thr
        eqf = jnp.where(eq, 1.0, 0.0)
        rank = jnp.dot(earlier, eqf.astype(BF16), preferred_element_type=F32) + carry
        keep = ((key > thr) | (eq & (rank < need))) & (x > -jnp.inf)
        sq.idx[_key_rows(kt), :] = jnp.where(keep, 1.0, -jnp.inf)
        return carry + jnp.sum(eqf, axis=0, keepdims=True)

    lax.fori_loop(0, nkt, mark, jnp.zeros((1, TQ), F32))
    sq.lo[...] = jnp.full((1, TQ), 0.5, F32)


def _attend(nkt, seqs):
    for sq in seqs:
        for hh in range(ATTN_HEADS):
            g, jj = divmod(hh, GROUP)
            sq.qs[g, jj * TQ:(jj + 1) * TQ, :] = sq.q[:, hh * HEAD_DIM:(hh + 1) * HEAD_DIM]
        sq.m[...] = jnp.full(sq.m.shape, M_INIT, F32)
        sq.l[...] = jnp.zeros(sq.l.shape, F32)
        sq.o[...] = jnp.zeros(sq.o.shape, F32)
    los = [sq.lo[...] for sq in seqs]

    def key_tile(kt, carry):
        rows = _key_rows(kt)
        for sq, lo in zip(seqs, los):
            bias = jnp.where(sq.idx[rows, :] >= lo, 0.0, NEG_MASK)
            scores = [lax.dot_general(sq.k[rows, g * HEAD_DIM:(g + 1) * HEAD_DIM], sq.qs[g], _NT,
                                      preferred_element_type=F32) for g in range(KV_HEADS)]
            for g in range(KV_HEADS):
                probs, alphas = [], []
                for jj in range(GROUP):
                    hh = g * GROUP + jj
                    s = scores[g][:, jj * TQ:(jj + 1) * TQ] + bias
                    m_old = sq.m[hh]
                    m_new = jnp.maximum(m_old, jnp.max(s, axis=0, keepdims=True))
                    alphas.append(jnp.exp2(m_old - m_new))
                    probs.append(jnp.exp2(s - m_new).astype(BF16))
                    sq.m[hh] = m_new
                pv = jnp.dot(sq.vt[kt, g * VT_ROWS:(g + 1) * VT_ROWS, :], jnp.concatenate(probs, axis=1),
                             preferred_element_type=F32)
                for jj in range(GROUP):
                    hh = g * GROUP + jj
                    cols = slice(jj * TQ, (jj + 1) * TQ)
                    orow = slice(hh * HEAD_DIM, (hh + 1) * HEAD_DIM)
                    sq.o[orow, :] = sq.o[orow, :] * alphas[jj] + pv[0:HEAD_DIM, cols]
                    sq.l[hh] = sq.l[hh] * alphas[jj] + pv[HEAD_DIM:HEAD_DIM + 1, cols]
        return carry

    lax.fori_loop(0, nkt, key_tile, 0)
    for sq in seqs:
        for hh in range(ATTN_HEADS):
            orow = slice(hh * HEAD_DIM, (hh + 1) * HEAD_DIM)
            sq.o[orow, :] = sq.o[orow, :] / sq.l[hh]
        sq.y[...] = sq.o[...].T.astype(BF16)


def _attn_kernel(seq, k_sel, *refs):
    seqs = [_Seq(*(r.at[s] for r in refs)) for s in range(SEQ_TILE)]
    j = pl.program_id(1)
    nkt = j + 1
    q_start = j * TQ

    @pl.when(j == 0)
    def _():
        for sq in seqs:
            for kt in range(seq // KT):
                v_t = sq.v[kt * KT:(kt + 1) * KT, :].astype(F32).T.astype(BF16)
                for g in range(KV_HEADS):
                    sq.vt[kt, g * VT_ROWS:g * VT_ROWS + HEAD_DIM, :] = v_t[g * HEAD_DIM:(g + 1) * HEAD_DIM, :]
                    sq.vt[kt, g * VT_ROWS + HEAD_DIM:(g + 1) * VT_ROWS, :] = jnp.ones(
                        (VT_ROWS - HEAD_DIM, KT), BF16)

    all_selected = (j + 1) * TQ <= k_sel

    @pl.when(all_selected)
    def _():
        def fill(kt, carry):
            kpos = lax.broadcasted_iota(I32, (KT, TQ), 0) + kt * KT
            qpos = lax.broadcasted_iota(I32, (KT, TQ), 1) + q_start
            for sq in seqs:
                sq.idx[_key_rows(kt), :] = jnp.where(kpos <= qpos, 1.0, -jnp.inf)
            return carry
        lax.fori_loop(0, nkt, fill, 0)
        for sq in seqs:
            sq.lo[...] = jnp.full((1, TQ), 0.5, F32)

    @pl.when(jnp.logical_not(all_selected))
    def _():
        stats = _indexer(nkt, q_start, seqs)
        failed = _select(nkt, k_sel, q_start, seqs, stats)
        for sq, f in zip(seqs, failed):
            @pl.when(f > 0.0)
            def _(sq=sq):
                _select_exact(nkt, k_sel, sq)

    _attend(nkt, seqs)


def _attention(qi, wi, q, ki, k, v, batch, seq):
    nq = seq // TQ
    k_sel = min(INDEX_TOPK, seq // 4)
    per_seq = lambda a: a.reshape(batch, seq, a.shape[-1])
    tile = lambda b, j: (b, j, 0)
    whole = lambda b, j: (b, 0, 0)
    scratch = lambda shape, dtype: pltpu.VMEM((SEQ_TILE,) + shape, dtype)
    y = pl.pallas_call(
        functools.partial(_attn_kernel, seq, k_sel),
        out_shape=jax.ShapeDtypeStruct((batch, seq, ATTN_WIDTH), BF16),
        grid=(batch // SEQ_TILE, nq),
        in_specs=[
            pl.BlockSpec((SEQ_TILE, TQ, IDX_HEADS * IDX_DIM), tile),
            pl.BlockSpec((SEQ_TILE, TQ, LANES), tile),
            pl.BlockSpec((SEQ_TILE, TQ, ATTN_WIDTH), tile),
            pl.BlockSpec((SEQ_TILE, seq, LANES), whole),
            pl.BlockSpec((SEQ_TILE, seq, LANES), whole),
            pl.BlockSpec((SEQ_TILE, seq, LANES), whole),
        ],
        out_specs=pl.BlockSpec((SEQ_TILE, TQ, ATTN_WIDTH), tile),
        scratch_shapes=[
            scratch((seq, TQ), F32),
            scratch((1, TQ), F32),
            scratch((IDX_HEADS, TQ, IDX_DIM), BF16),
            scratch((KV_HEADS, GROUP * TQ, HEAD_DIM), BF16),
            scratch((seq // KT, KV_HEADS * VT_ROWS, KT), BF16),
            scratch((ATTN_HEADS, 1, TQ), F32),
            scratch((ATTN_HEADS, 1, TQ), F32),
            scratch((ATTN_WIDTH, TQ), F32),
        ],
        compiler_params=pltpu.CompilerParams(
            dimension_semantics=("arbitrary", "arbitrary"), vmem_limit_bytes=VMEM_LIMIT_BYTES),
        name="dsa_attention",
    )(per_seq(qi), per_seq(wi), per_seq(q), per_seq(ki), per_seq(k), per_seq(v))
    return y.reshape(batch * seq, ATTN_WIDTH)


def _mix_kernel(y_ref, uc_ref, uch_ref, up_ref, uph_ref, h_ref, dw_ref, cb_ref, lg_ref, lb_ref,
                pw_ref, plw_ref, psc_ref, wo_ref, gpost_ref, o_ref, ypad_ref, upad_ref):
    j = pl.program_id(1)
    has_prev = (j > 0).astype(F32)

    def glu(u):
        return u[:, :CONV_WIDTH] * jax.nn.sigmoid(u[:, CONV_WIDTH:])

    def fill_shifted(ref, halo_rows, tile_rows):
        ref[0, 0:HALO, :] = halo_rows
        ref[0, HALO:HALO + TM, :] = tile_rows
        for s in range(1, SUBLANES):
            ref[s, 0:HALO + TM - SUBLANES, :] = ref[0, s:s + HALO + TM - SUBLANES, :]

    def rows_from(ref, off):
        s = off % SUBLANES
        return ref[s, off - s:off - s + TM, :]

    fill_shifted(ypad_ref, glu(uch_ref[...]) * has_prev, glu(uc_ref[...]))
    acc = None
    for tap in range(CONV_KERNEL):
        term = rows_from(ypad_ref, HALO - (CONV_KERNEL - 1) + tap) * dw_ref[tap:tap + 1, :]
        acc = term if acc is None else acc + term
    yc = acc + cb_ref[...]
    mu = jnp.mean(yc, axis=-1, keepdims=True)
    var = jnp.mean(jnp.square(yc - mu), axis=-1, keepdims=True)
    yn = (yc - mu) * lax.rsqrt(var + LN_EPS) * lg_ref[...] + lb_ref[...]
    ys = yn * jax.nn.sigmoid(yn)
    y_conv = jnp.dot(ys.astype(BF16), pw_ref[...], preferred_element_type=F32)

    u0 = up_ref[...]
    fill_shifted(upad_ref, uph_ref[...] * has_prev, u0)

    def back(i):
        return rows_from(upad_ref, HALO - i)

    sums = []
    run = u0
    nxt = 1
    for w in POOL_WINDOWS:
        while nxt < w:
            run = run + back(nxt)
            nxt += 1
        sums.append(run)
    lane = lax.broadcasted_iota(I32, (TM, POOL_WIDTH), 1)
    win_sum = sums[-1]
    win = jnp.full((TM, POOL_WIDTH), POOL_WINDOWS[-1], I32)
    for gi in range(POOL_GROUPS - 2, -1, -1):
        in_group = lane < (gi + 1) * POOL_GROUP_DIM
        win_sum = jnp.where(in_group, sums[gi], win_sum)
        win = jnp.where(in_group, POOL_WINDOWS[gi], win)
    t = lax.broadcasted_iota(I32, (TM, POOL_WIDTH), 0) + j * TM
    count = jnp.minimum(t + 1, win).astype(F32)
    pooled = win_sum / count - u0
    y_pool = jnp.dot(pooled.astype(BF16), plw_ref[...], preferred_element_type=F32) * psc_ref[...]

    c0, c1 = ATTN_WIDTH, ATTN_WIDTH + CONV_WIDTH
    mix = (jnp.dot(y_ref[...], wo_ref[0:c0, :], preferred_element_type=F32)
           + jnp.dot(y_conv.astype(BF16), wo_ref[c0:c1, :], preferred_element_type=F32)
           + jnp.dot(y_pool.astype(BF16), wo_ref[c1:, :], preferred_element_type=F32))
    o_ref[...] = h_ref[...] + _rms(mix, gpost_ref[...])


def _mix(layer, y, uc, up, h, dw, cb, lg, lb, pw, plw, psc, wo, gpost, batch, seq):
    T = batch * seq
    nt = seq // TM
    tile = lambda b, j: (b * nt + j, 0)
    halo_per_tile = TM // HALO
    halo = lambda b, j: (b * (seq // HALO) + jnp.maximum(j * halo_per_tile - 1, 0), 0)
    return pl.pallas_call(
        _mix_kernel,
        out_shape=jax.ShapeDtypeStruct((T, D_MODEL), F32),
        grid=(batch, nt),
        in_specs=[
            pl.BlockSpec((TM, ATTN_WIDTH), tile),
            pl.BlockSpec((TM, 2 * CONV_WIDTH), tile),
            pl.BlockSpec((HALO, 2 * CONV_WIDTH), halo),
            pl.BlockSpec((TM, POOL_WIDTH), tile),
            pl.BlockSpec((HALO, POOL_WIDTH), halo),
            pl.BlockSpec((TM, D_MODEL), tile),
            _const_spec((HALO, CONV_WIDTH)),
            _const_spec((1, CONV_WIDTH)),
            _const_spec((1, CONV_WIDTH)),
            _const_spec((1, CONV_WIDTH)),
            _layer_spec(layer, (CONV_WIDTH, CONV_WIDTH)),
            _layer_spec(layer, (POOL_WIDTH, POOL_WIDTH)),
            _const_spec((1, POOL_WIDTH)),
            _layer_spec(layer, (D_MODEL, D_MODEL)),
            _const_spec((1, D_MODEL)),
        ],
        out_specs=pl.BlockSpec((TM, D_MODEL), tile),
        scratch_shapes=[
            pltpu.VMEM((SUBLANES, HALO + TM, CONV_WIDTH), F32),
            pltpu.VMEM((SUBLANES, HALO + TM, POOL_WIDTH), F32),
        ],
        compiler_params=pltpu.CompilerParams(
            dimension_semantics=("parallel", "parallel"), vmem_limit_bytes=VMEM_LIMIT_BYTES),
        name="mix_out",
    )(y, uc, uc, up, up, h, dw, cb, lg, lb, pw, plw, psc, wo, gpost)


FF_CHUNK = 1024


def _mlp_kernel(h_ref, p_ref, gpre_ref, wup_ref, wdn_ref, gpost_ref, wg_ref, wp_ref, o_ref):
    h = h_ref[...]
    m = _rms(h, gpre_ref[...]).astype(BF16)
    acc = None
    for c in range(D_FF // FF_CHUNK):
        cs = slice(c * FF_CHUNK, (c + 1) * FF_CHUNK)
        r = jnp.maximum(jnp.dot(m, wup_ref[:, cs], preferred_element_type=F32), 0.0)
        d = jnp.dot((r * r).astype(BF16), wdn_ref[cs, :], preferred_element_type=F32)
        acc = d if acc is None else acc + d
    h2 = h + _rms(acc, gpost_ref[...])
    gate = jax.nn.sigmoid(jnp.dot(h2.astype(BF16), wg_ref[...], preferred_element_type=F32))
    emb = jnp.dot(p_ref[...].astype(BF16), wp_ref[...], preferred_element_type=F32)
    o_ref[...] = h2 + gate * emb


def _mlp(layer, h, p, gpre, wup, wdn, gpost, wg, wp):
    T = h.shape[0]
    row = lambda i: (i, 0)
    return pl.pallas_call(
        _mlp_kernel,
        out_shape=jax.ShapeDtypeStruct((T, D_MODEL), F32),
        grid=(T // TM,),
        in_specs=[
            pl.BlockSpec((TM, D_MODEL), row),
            pl.BlockSpec((None, TM, PLE_DIM), lambda i: (layer, i, 0)),
            _const_spec((1, D_MODEL)),
            _layer_spec(layer, (D_MODEL, D_FF)),
            _layer_spec(layer, (D_FF, D_MODEL)),
            _const_spec((1, D_MODEL)),
            _layer_spec(layer, (D_MODEL, D_MODEL)),
            _layer_spec(layer, (PLE_DIM, D_MODEL)),
        ],
        out_specs=pl.BlockSpec((TM, D_MODEL), row),
        compiler_params=pltpu.CompilerParams(
            dimension_semantics=("parallel",), vmem_limit_bytes=VMEM_LIMIT_BYTES),
        name="mlp_ple",
    )(h, p, gpre, wup, wdn, gpost, wg, wp)


def _lane_constants():
    inv_freq = ROPE_THETA ** (-jnp.arange(0, ROPE_DIM, 2, dtype=F32) / ROPE_DIM)
    lane = np.arange(LANES) % HEAD_DIM
    rot = lane < ROPE_DIM
    freq = jnp.where(jnp.asarray(rot), inv_freq[lane % ROPE_HALF], 0.0)
    lo = jnp.asarray(np.where(lane < ROPE_HALF, -1.0, 0.0), F32)
    hi = jnp.asarray(np.where(rot & (lane >= ROPE_HALF), 1.0, 0.0), F32)
    return jnp.concatenate([jnp.stack([freq, lo, hi]), jnp.zeros((5, LANES), F32)], axis=0)


def kernel(x, p, positions, norm_mix_pre, w_in, conv_dw, conv_b, conv_ln_g, conv_ln_b, conv_pw,
           pool_w, pool_scale, w_out, norm_mix_post, norm_mlp_pre, w_up, w_down, norm_mlp_post,
           ple_proj, ple_gate):
    batch, seq, d_model = x.shape
    depth = w_in.shape[0]
    assert d_model == D_MODEL and seq % TQ == 0 and seq % TM == 0 and TQ == KT and batch % SEQ_TILE == 0
    T = batch * seq

    w_in_pad = jnp.concatenate(
        [w_in[:, :, :IN_UNPADDED_SPLIT],
         jnp.zeros((depth, D_MODEL, COL_CONV - IN_UNPADDED_SPLIT), w_in.dtype),
         w_in[:, :, IN_UNPADDED_SPLIT:]], axis=2).astype(BF16)
    dw_pad = jnp.concatenate([conv_dw, jnp.zeros((depth, HALO - CONV_KERNEL, CONV_WIDTH), F32)], axis=1)
    pool_bd = jnp.einsum('lgcd,gh->lgchd', pool_w, jnp.eye(POOL_GROUPS, dtype=F32)).reshape(
        depth, POOL_WIDTH, POOL_WIDTH).astype(BF16)
    conv_pw_b = conv_pw.astype(BF16)
    w_out_b = w_out.astype(BF16)
    w_up_b = w_up.astype(BF16)
    w_down_b = w_down.astype(BF16)
    gate_b = ple_gate.astype(BF16)
    proj_b = ple_proj.astype(BF16)
    vec = lambda a, i: a[i].reshape(1, -1)

    lane_consts = _lane_constants()
    pos = positions.reshape(T, 1)
    h = x.reshape(T, D_MODEL)
    p2 = p.reshape(depth, T, PLE_DIM)
    for i in range(depth):
        q, k, v, qi, ki, wi, uc, up = _inproj(i, h, pos, vec(norm_mix_pre, i), w_in_pad, lane_consts)
        y = _attention(qi, wi, q, ki, k, v, batch, seq)
        h = _mix(i, y, uc, up, h, dw_pad[i], vec(conv_b, i), vec(conv_ln_g, i), vec(conv_ln_b, i),
                 conv_pw_b, pool_bd, vec(pool_scale, i), w_out_b, vec(norm_mix_post, i), batch, seq)
        h = _mlp(i, h, p2, vec(norm_mlp_pre, i), w_up_b, w_down_b, vec(norm_mlp_post, i),
                 gate_b, proj_b)
    return h.reshape(batch, seq, D_MODEL)
```

```python
import collections
import functools

import numpy as np
import jax
import jax.numpy as jnp
from jax import lax
from jax.experimental import pallas as pl
from jax.experimental.pallas import tpu as pltpu

F32 = jnp.float32
BF16 = jnp.bfloat16
I32 = jnp.int32

D_MODEL = 1024
HEAD_DIM = 64
ATTN_WIDTH = 512
ATTN_HEADS = 8
KV_HEADS = 2
GROUP = ATTN_HEADS // KV_HEADS
IDX_HEADS = 16
IDX_DIM = 64
INDEX_TOPK = 256
CONV_WIDTH = 256
CONV_KERNEL = 31
POOL_WIDTH = 256
POOL_GROUPS = 4
POOL_GROUP_DIM = 64
POOL_WINDOWS = (2, 4, 8, 16)
D_FF = 4096
PLE_DIM = 256
ROPE_THETA = 500000.0
ROPE_DIM = 16
ROPE_HALF = ROPE_DIM // 2
NORM_EPS = 1e-6
LN_EPS = 1e-5
ATT_SCALE = HEAD_DIM ** -0.5
Q_SCALE = ATT_SCALE * float(np.log2(np.e))
IDX_SCALE = (IDX_DIM ** -0.5) * (IDX_HEADS ** -0.5)

LANES = 128
SUBLANES = 8
VMEM_LIMIT_BYTES = 56 * 1024 * 1024

COL_Q = 0
COL_K = COL_Q + ATTN_WIDTH
COL_V = COL_K + KV_HEADS * HEAD_DIM
COL_QI = COL_V + KV_HEADS * HEAD_DIM
COL_KIWI = COL_QI + IDX_HEADS * IDX_DIM
COL_CONV = COL_KIWI + LANES
COL_POOL = COL_CONV + 2 * CONV_WIDTH
IN_PAD_WIDTH = COL_POOL + POOL_WIDTH
IN_UNPADDED_SPLIT = ATTN_WIDTH + 2 * KV_HEADS * HEAD_DIM + IDX_HEADS * IDX_DIM + IDX_DIM + IDX_HEADS
WI_LANE0 = IDX_DIM

TM = 512
TQ = 256
KT = 256
KH = 128
SEQ_TILE = 4
HALO = 32
SEARCH_UNROLL = 4
COUNT_CHAINS = 8
SEARCH_ROUNDS = 12
VT_ROWS = HEAD_DIM + 16

NEG_MASK = -1e30
M_INIT = -2e30
INT_MIN = -2 ** 31


def _const_spec(shape):
    zeros = (0,) * len(shape)
    return pl.BlockSpec(shape, lambda *_: zeros, pipeline_mode=pl.Buffered(1))


def _layer_spec(layer, shape):
    index = (layer,) + (0,) * len(shape)
    return pl.BlockSpec((None,) + tuple(shape), lambda *_: index, pipeline_mode=pl.Buffered(1))


def _rms(x, gain):
    return x * lax.rsqrt(jnp.mean(x * x, axis=-1, keepdims=True) + NORM_EPS) * gain


def _inproj_kernel(h_ref, pos_ref, g_ref, w_ref, lc_ref,
                   q_ref, k_ref, v_ref, qi_ref, ki_ref, wi_ref, uc_ref, up_ref):
    a = _rms(h_ref[...], g_ref[...]).astype(BF16)
    pos = pos_ref[...].astype(F32)
    ang = pos * lc_ref[0:1, :]
    cos = jnp.cos(ang)
    sin = jnp.sin(ang)
    s_lo = sin * lc_ref[1:2, :]
    s_hi = sin * lc_ref[2:3, :]

    def rope(x):
        return x * cos + pltpu.roll(x, LANES - ROPE_HALF, 1) * s_lo + pltpu.roll(x, ROPE_HALF, 1) * s_hi

    def proj(c0, c1):
        return jnp.dot(a, w_ref[:, c0:c1], preferred_element_type=F32)

    u = proj(COL_Q, COL_K)
    for s in range(ATTN_WIDTH // LANES):
        sl = slice(s * LANES, (s + 1) * LANES)
        q_ref[:, sl] = (rope(u[:, sl]) * Q_SCALE).astype(BF16)
    u = proj(COL_K, COL_QI)
    k_ref[...] = rope(u[:, :LANES]).astype(BF16)
    v_ref[...] = u[:, LANES:].astype(BF16)
    u = proj(COL_QI, COL_KIWI)
    for s in range(IDX_HEADS * IDX_DIM // LANES):
        sl = slice(s * LANES, (s + 1) * LANES)
        qi_ref[:, sl] = rope(u[:, sl]).astype(BF16)
    u = proj(COL_KIWI, COL_CONV)
    ki_ref[...] = rope(u).astype(BF16)
    wi_ref[...] = u * IDX_SCALE
    uc_ref[...] = proj(COL_CONV, COL_POOL)
    up_ref[...] = proj(COL_POOL, IN_PAD_WIDTH)


def _inproj(layer, h, pos, gain, w, lane_consts):
    T = h.shape[0]
    row = lambda i: (i, 0)
    out_shape = (
        jax.ShapeDtypeStruct((T, ATTN_WIDTH), BF16),
        jax.ShapeDtypeStruct((T, LANES), BF16),
        jax.ShapeDtypeStruct((T, LANES), BF16),
        jax.ShapeDtypeStruct((T, IDX_HEADS * IDX_DIM), BF16),
        jax.ShapeDtypeStruct((T, LANES), BF16),
        jax.ShapeDtypeStruct((T, LANES), F32),
        jax.ShapeDtypeStruct((T, 2 * CONV_WIDTH), F32),
        jax.ShapeDtypeStruct((T, POOL_WIDTH), F32),
    )
    return pl.pallas_call(
        _inproj_kernel,
        out_shape=out_shape,
        grid=(T // TM,),
        in_specs=[
            pl.BlockSpec((TM, D_MODEL), row),
            pl.BlockSpec((TM, 1), row),
            _const_spec((1, D_MODEL)),
            _layer_spec(layer, (D_MODEL, IN_PAD_WIDTH)),
            _const_spec((8, LANES)),
        ],
        out_specs=tuple(pl.BlockSpec((TM, s.shape[1]), row) for s in out_shape),
        compiler_params=pltpu.CompilerParams(
            dimension_semantics=("parallel",), vmem_limit_bytes=VMEM_LIMIT_BYTES),
        name="inproj",
    )(h, pos, gain, w, lane_consts)


_NT = (((1,), (1,)), ((), ()))

_Seq = collections.namedtuple(
    "_Seq", "qi wi q ki k v y idx lo qis qs vt m l o")


def _key_rows(kt):
    return pl.ds(pl.multiple_of(kt * KT, KT), KT)


def _sortable(x):
    bits = pltpu.bitcast(x, I32)
    return bits ^ ((bits >> 31) & 0x7FFFFFFF)


def _indexer(nkt, q_start, seqs):
    w_ts = []
    for sq in seqs:
        for hh in range(IDX_HEADS):
            sq.qis[hh] = sq.qi[:, hh * IDX_DIM:(hh + 1) * IDX_DIM]
        w_ts.append(sq.wi[...].T)

    def key_tile(kt, carry):
        out = []
        for sq, w_t, (cmax, cmin) in zip(seqs, w_ts, carry):
            for half in range(KT // KH):
                rows = pl.ds(pl.multiple_of(kt * KT + half * KH, KH), KH)
                ki = sq.ki[rows, 0:IDX_DIM]
                acc = None
                for hh in range(IDX_HEADS):
                    d = lax.dot_general(ki, sq.qis[hh], _NT, preferred_element_type=F32)
                    term = jnp.maximum(d, 0.0) * w_t[WI_LANE0 + hh:WI_LANE0 + hh + 1, :]
                    acc = term if acc is None else acc + term
                kpos = lax.broadcasted_iota(I32, (KH, TQ), 0) + (kt * KT + half * KH)
                qpos = lax.broadcasted_iota(I32, (KH, TQ), 1) + q_start
                blk = jnp.where(kpos <= qpos, acc, -jnp.inf)
                sq.idx[rows, :] = blk
                cmax = jnp.maximum(cmax, jnp.max(blk, axis=0, keepdims=True))
                cmin = jnp.minimum(
                    cmin, jnp.min(jnp.where(blk > -jnp.inf, blk, jnp.inf), axis=0, keepdims=True))
            out.append((cmax, cmin))
        return tuple(out)

    row = lambda v: jnp.full((1, TQ), v, F32)
    return lax.fori_loop(0, nkt, key_tile, tuple((row(-jnp.inf), row(jnp.inf)) for _ in seqs))


def _select(nkt, k_sel, q_start, seqs, stats):
    kf = float(k_sel)

    def count_ge(ts):
        tbs = [jnp.broadcast_to(t, (SUBLANES, TQ)) for t in ts]

        def add_rows(start, n_rows, accs):
            out = []
            for sq, tb, acc in zip(seqs, tbs, accs):
                tiles = sq.idx.at[pl.ds(start, n_rows), :]
                chains = [None] * COUNT_CHAINS
                for r in range(n_rows // SUBLANES):
                    hit = jnp.where(tiles[r * SUBLANES:(r + 1) * SUBLANES, :] >= tb, 1.0, 0.0)
                    c = r % COUNT_CHAINS
                    chains[c] = hit if chains[c] is None else chains[c] + hit
                while len(chains) > 1:
                    chains = [chains[a] + chains[a + 1] for a in range(0, len(chains), 2)]
                out.append(acc + chains[0])
            return tuple(out)

        accs = lax.fori_loop(
            0, nkt // 2, lambda i, a: add_rows(pl.multiple_of(i * (2 * KT), 2 * KT), 2 * KT, a),
            tuple(jnp.zeros((SUBLANES, TQ), F32) for _ in seqs))
        accs = lax.cond(nkt % 2 == 1,
                        lambda a: add_rows(pl.multiple_of((nkt - 1) * KT, KT), KT, a), lambda a: a, accs)
        return [jnp.sum(acc, axis=0, keepdims=True) for acc in accs]

    def step(states):
        mids = [lo + (hi - lo) * 0.5 for lo, hi, _ in states]
        out = []
        for (lo, hi, cnt_lo), mid, cnt in zip(states, mids, count_ge(mids)):
            up = cnt >= kf
            out.append((jnp.where(up, mid, lo), jnp.where(up, hi, mid), jnp.where(up, cnt, cnt_lo)))
        return out

    def unsettled(cnt_lo):
        return jnp.max(jnp.where(cnt_lo != kf, 1.0, 0.0))

    n = len(seqs)

    def cond(c):
        pending = c[1]
        for f in c[2:n + 1]:
            pending = jnp.maximum(pending, f)
        return (c[0] < SEARCH_ROUNDS) & (pending > 0.0)

    def body(c):
        states = [c[n + 1 + 3 * s:n + 4 + 3 * s] for s in range(n)]
        for _ in range(SEARCH_UNROLL):
            states = step(states)
        return (c[0] + 1,) + tuple(unsettled(st[2]) for st in states) + tuple(x for st in states for x in st)

    n_causal = (lax.broadcasted_iota(I32, (1, TQ), 1) + (q_start + 1)).astype(F32)
    init = [(cmin, cmax, n_causal) for cmax, cmin in stats]
    out = lax.while_loop(cond, body, (jnp.int32(0),) + tuple(unsettled(st[2]) for st in init)
                         + tuple(x for st in init for x in st))
    for s, sq in enumerate(seqs):
        sq.lo[...] = out[n + 1 + 3 * s]
    return out[1:n + 1]


def _select_exact(nkt, k_sel, sq):
    def over_tiles(fn, init):
        return lax.fori_loop(0, nkt, lambda kt, c: fn(sq.idx[_key_rows(kt), :], c), init)

    def count(pred):
        return over_tiles(
            lambda x, c: c + jnp.sum(pred(_sortable(x)).astype(I32), axis=0, keepdims=True),
            jnp.zeros((1, TQ), I32))

    def search(i, prefix):
        cand = prefix | lax.shift_left(jnp.int32(1), 31 - i)
        return jnp.where(count(lambda key: key >= (cand ^ INT_MIN)) >= k_sel, cand, prefix)

    thr = lax.fori_loop(0, 32, search, jnp.zeros((1, TQ), I32)) ^ INT_MIN
    need = (k_sel - count(lambda key: key > thr)).astype(F32)
    r = lax.broadcasted_iota(I32, (KT, KT), 0)
    c = lax.broadcasted_iota(I32, (KT, KT), 1)
    earlier = (c < r).astype(BF16)

    def mark(kt, carry):
        x = sq.idx[_key_rows(kt), :]
        key = _sortable(x)
        eq = key == thr
        eqf = jnp.where(eq, 1.0, 0.0)
        rank = jnp.dot(earlier, eqf.astype(BF16), preferred_element_type=F32) + carry
        keep = ((key > thr) | (eq & (rank < need))) & (x > -jnp.inf)
        sq.idx[_key_rows(kt), :] = jnp.where(keep, 1.0, -jnp.inf)
        return carry + jnp.sum(eqf, axis=0, keepdims=True)

    lax.fori_loop(0, nkt, mark, jnp.zeros((1, TQ), F32))
    sq.lo[...] = jnp.full((1, TQ), 0.5, F32)


def _attend(nkt, seqs):
    for sq in seqs:
        for hh in range(ATTN_HEADS):
            g, jj = divmod(hh, GROUP)
            sq.qs[g, jj * TQ:(jj + 1) * TQ, :] = sq.q[:, hh * HEAD_DIM:(hh + 1) * HEAD_DIM]
        sq.m[...] = jnp.full(sq.m.shape, M_INIT, F32)
        sq.l[...] = jnp.zeros(sq.l.shape, F32)
        sq.o[...] = jnp.zeros(sq.o.shape, F32)
    los = [sq.lo[...] for sq in seqs]

    def key_tile(kt, carry):
        rows = _key_rows(kt)
        for sq, lo in zip(seqs, los):
            bias = jnp.where(sq.idx[rows, :] >= lo, 0.0, NEG_MASK)
            scores = [lax.dot_general(sq.k[rows, g * HEAD_DIM:(g + 1) * HEAD_DIM], sq.qs[g], _NT,
                                      preferred_element_type=F32) for g in range(KV_HEADS)]
            for g in range(KV_HEADS):
                probs, alphas = [], []
                for jj in range(GROUP):
                    hh = g * GROUP + jj
                    s = scores[g][:, jj * TQ:(jj + 1) * TQ] + bias
                    m_old = sq.m[hh]
                    m_new = jnp.maximum(m_old, jnp.max(s, axis=0, keepdims=True))
                    alphas.append(jnp.exp2(m_old - m_new))
                    probs.append(jnp.exp2(s - m_new).astype(BF16))
                    sq.m[hh] = m_new
                pv = jnp.dot(sq.vt[kt, g * VT_ROWS:(g + 1) * VT_ROWS, :], jnp.concatenate(probs, axis=1),
                             preferred_element_type=F32)
                for jj in range(GROUP):
                    hh = g * GROUP + jj
                    cols = slice(jj * TQ, (jj + 1) * TQ)
                    orow = slice(hh * HEAD_DIM, (hh + 1) * HEAD_DIM)
                    sq.o[orow, :] = sq.o[orow, :] * alphas[jj] + pv[0:HEAD_DIM, cols]
                    sq.l[hh] = sq.l[hh] * alphas[jj] + pv[HEAD_DIM:HEAD_DIM + 1, cols]
        return carry

    lax.fori_loop(0, nkt, key_tile, 0)
    for sq in seqs:
        for hh in range(ATTN_HEADS):
            orow = slice(hh * HEAD_DIM, (hh + 1) * HEAD_DIM)
            sq.o[orow, :] = sq.o[orow, :] / sq.l[hh]
        sq.y[...] = sq.o[...].T.astype(BF16)


def _attn_kernel(seq, k_sel, *refs):
    seqs = [_Seq(*(r.at[s] for r in refs)) for s in range(SEQ_TILE)]
    j = pl.program_id(1)
    nkt = j + 1
    q_start = j * TQ

    @pl.when(j == 0)
    def _():
        for sq in seqs:
            for kt in range(seq // KT):
                v_t = sq.v[kt * KT:(kt + 1) * KT, :].astype(F32).T.astype(BF16)
                for g in range(KV_HEADS):
                    sq.vt[kt, g * VT_ROWS:g * VT_ROWS + HEAD_DIM, :] = v_t[g * HEAD_DIM:(g + 1) * HEAD_DIM, :]
                    sq.vt[kt, g * VT_ROWS + HEAD_DIM:(g + 1) * VT_ROWS, :] = jnp.ones(
                        (VT_ROWS - HEAD_DIM, KT), BF16)

    all_selected = (j + 1) * TQ <= k_sel

    @pl.when(all_selected)
    def _():
        def fill(kt, carry):
            kpos = lax.broadcasted_iota(I32, (KT, TQ), 0) + kt * KT
            qpos = lax.broadcasted_iota(I32, (KT, TQ), 1) + q_start
            for sq in seqs:
                sq.idx[_key_rows(kt), :] = jnp.where(kpos <= qpos, 1.0, -jnp.inf)
            return carry
        lax.fori_loop(0, nkt, fill, 0)
        for sq in seqs:
            sq.lo[...] = jnp.full((1, TQ), 0.5, F32)

    @pl.when(jnp.logical_not(all_selected))
    def _():
        stats = _indexer(nkt, q_start, seqs)
        failed = _select(nkt, k_sel, q_start, seqs, stats)
        for sq, f in zip(seqs, failed):
            @pl.when(f > 0.0)
            def _(sq=sq):
                _select_exact(nkt, k_sel, sq)

    _attend(nkt, seqs)


def _attention(qi, wi, q, ki, k, v, batch, seq):
    nq = seq // TQ
    k_sel = min(INDEX_TOPK, seq // 4)
    per_seq = lambda a: a.reshape(batch, seq, a.shape[-1])
    tile = lambda b, j: (b, j, 0)
    whole = lambda b, j: (b, 0, 0)
    scratch = lambda shape, dtype: pltpu.VMEM((SEQ_TILE,) + shape, dtype)
    y = pl.pallas_call(
        functools.partial(_attn_kernel, seq, k_sel),
        out_shape=jax.ShapeDtypeStruct((batch, seq, ATTN_WIDTH), BF16),
        grid=(batch // SEQ_TILE, nq),
        in_specs=[
            pl.BlockSpec((SEQ_TILE, TQ, IDX_HEADS * IDX_DIM), tile),
            pl.BlockSpec((SEQ_TILE, TQ, LANES), tile),
            pl.BlockSpec((SEQ_TILE, TQ, ATTN_WIDTH), tile),
            pl.BlockSpec((SEQ_TILE, seq, LANES), whole),
            pl.BlockSpec((SEQ_TILE, seq, LANES), whole),
            pl.BlockSpec((SEQ_TILE, seq, LANES), whole),
        ],
        out_specs=pl.BlockSpec((SEQ_TILE, TQ, ATTN_WIDTH), tile),
        scratch_shapes=[
            scratch((seq, TQ), F32),
            scratch((1, TQ), F32),
            scratch((IDX_HEADS, TQ, IDX_DIM), BF16),
            scratch((KV_HEADS, GROUP * TQ, HEAD_DIM), BF16),
            scratch((seq // KT, KV_HEADS * VT_ROWS, KT), BF16),
            scratch((ATTN_HEADS, 1, TQ), F32),
            scratch((ATTN_HEADS, 1, TQ), F32),
            scratch((ATTN_WIDTH, TQ), F32),
        ],
        compiler_params=pltpu.CompilerParams(
            dimension_semantics=("arbitrary", "arbitrary"), vmem_limit_bytes=VMEM_LIMIT_BYTES),
        name="dsa_attention",
    )(per_seq(qi), per_seq(wi), per_seq(q), per_seq(ki), per_seq(k), per_seq(v))
    return y.reshape(batch * seq, ATTN_WIDTH)


def _mix_kernel(y_ref, uc_ref, uch_ref, up_ref, uph_ref, h_ref, dw_ref, cb_ref, lg_ref, lb_ref,
                pw_ref, plw_ref, psc_ref, wo_ref, gpost_ref, o_ref, ypad_ref, upad_ref):
    j = pl.program_id(1)
    has_prev = (j > 0).astype(F32)

    def glu(u):
        return u[:, :CONV_WIDTH] * jax.nn.sigmoid(u[:, CONV_WIDTH:])

    def fill_shifted(ref, halo_rows, tile_rows):
        ref[0, 0:HALO, :] = halo_rows
        ref[0, HALO:HALO + TM, :] = tile_rows
        for s in range(1, SUBLANES):
            ref[s, 0:HALO + TM - SUBLANES, :] = ref[0, s:s + HALO + TM - SUBLANES, :]

    def rows_from(ref, off):
        s = off % SUBLANES
        return ref[s, off - s:off - s + TM, :]

    fill_shifted(ypad_ref, glu(uch_ref[...]) * has_prev, glu(uc_ref[...]))
    acc = None
    for tap in range(CONV_KERNEL):
        term = rows_from(ypad_ref, HALO - (CONV_KERNEL - 1) + tap) * dw_ref[tap:tap + 1, :]
        acc = term if acc is None else acc + term
    yc = acc + cb_ref[...]
    mu = jnp.mean(yc, axis=-1, keepdims=True)
    var = jnp.mean(jnp.square(yc - mu), axis=-1, keepdims=True)
    yn = (yc - mu) * lax.rsqrt(var + LN_EPS) * lg_ref[...] + lb_ref[...]
    ys = yn * jax.nn.sigmoid(yn)
    y_conv = jnp.dot(ys.astype(BF16), pw_ref[...], preferred_element_type=F32)

    u0 = up_ref[...]
    end = HALO + TM
    upad_ref[0, 0:HALO, :] = uph_ref[...] * has_prev
    upad_ref[0, HALO:end, :] = u0
    sums = []
    for level, w in enumerate(POOL_WINDOWS, start=1):
        assert w == 2 ** level and SUBLANES * level <= HALO
        start, back = SUBLANES * level, w // 2
        doubled = upad_ref[level - 1, start:end, :] + upad_ref[level - 1, start - back:end - back, :]
        if level < len(POOL_WINDOWS):
            upad_ref[level, start:end, :] = doubled
        sums.append(doubled[HALO - start:, :])
    lane = lax.broadcasted_iota(I32, (TM, POOL_WIDTH), 1)
    win_sum = sums[-1]
    win = jnp.full((TM, POOL_WIDTH), POOL_WINDOWS[-1], I32)
    for gi in range(POOL_GROUPS - 2, -1, -1):
        in_group = lane < (gi + 1) * POOL_GROUP_DIM
        win_sum = jnp.where(in_group, sums[gi], win_sum)
        win = jnp.where(in_group, POOL_WINDOWS[gi], win)
    t = lax.broadcasted_iota(I32, (TM, POOL_WIDTH), 0) + j * TM
    count = jnp.minimum(t + 1, win).astype(F32)
    pooled = win_sum / count - u0
    y_pool = jnp.dot(pooled.astype(BF16), plw_ref[...], preferred_element_type=F32) * psc_ref[...]

    c0, c1 = ATTN_WIDTH, ATTN_WIDTH + CONV_WIDTH
    mix = (jnp.dot(y_ref[...], wo_ref[0:c0, :], preferred_element_type=F32)
           + jnp.dot(y_conv.astype(BF16), wo_ref[c0:c1, :], preferred_element_type=F32)
           + jnp.dot(y_pool.astype(BF16), wo_ref[c1:, :], preferred_element_type=F32))
    o_ref[...] = h_ref[...] + _rms(mix, gpost_ref[...])


def _mix(layer, y, uc, up, h, dw, cb, lg, lb, pw, plw, psc, wo, gpost, batch, seq):
    T = batch * seq
    nt = seq // TM
    tile = lambda b, j: (b * nt + j, 0)
    halo_per_tile = TM // HALO
    halo = lambda b, j: (b * (seq // HALO) + jnp.maximum(j * halo_per_tile - 1, 0), 0)
    return pl.pallas_call(
        _mix_kernel,
        out_shape=jax.ShapeDtypeStruct((T, D_MODEL), F32),
        grid=(batch, nt),
        in_specs=[
            pl.BlockSpec((TM, ATTN_WIDTH), tile),
            pl.BlockSpec((TM, 2 * CONV_WIDTH), tile),
            pl.BlockSpec((HALO, 2 * CONV_WIDTH), halo),
            pl.BlockSpec((TM, POOL_WIDTH), tile),
            pl.BlockSpec((HALO, POOL_WIDTH), halo),
            pl.BlockSpec((TM, D_MODEL), tile),
            _const_spec((HALO, CONV_WIDTH)),
            _const_spec((1, CONV_WIDTH)),
            _const_spec((1, CONV_WIDTH)),
            _const_spec((1, CONV_WIDTH)),
            _layer_spec(layer, (CONV_WIDTH, CONV_WIDTH)),
            _layer_spec(layer, (POOL_WIDTH, POOL_WIDTH)),
            _const_spec((1, POOL_WIDTH)),
            _layer_spec(layer, (D_MODEL, D_MODEL)),
            _const_spec((1, D_MODEL)),
        ],
        out_specs=pl.BlockSpec((TM, D_MODEL), tile),
        scratch_shapes=[
            pltpu.VMEM((SUBLANES, HALO + TM, CONV_WIDTH), F32),
            pltpu.VMEM((len(POOL_WINDOWS), HALO + TM, POOL_WIDTH), F32),
        ],
        compiler_params=pltpu.CompilerParams(
            dimension_semantics=("parallel", "parallel"), vmem_limit_bytes=VMEM_LIMIT_BYTES),
        name="mix_out",
    )(y, uc, uc, up, up, h, dw, cb, lg, lb, pw, plw, psc, wo, gpost)


FF_CHUNK = 1024


def _mlp_kernel(h_ref, p_ref, gpre_ref, wup_ref, wdn_ref, gpost_ref, wg_ref, wp_ref, o_ref):
    h = h_ref[...]
    m = _rms(h, gpre_ref[...]).astype(BF16)
    acc = None
    for c in range(D_FF // FF_CHUNK):
        cs = slice(c * FF_CHUNK, (c + 1) * FF_CHUNK)
        r = jnp.maximum(jnp.dot(m, wup_ref[:, cs], preferred_element_type=F32), 0.0)
        d = jnp.dot((r * r).astype(BF16), wdn_ref[cs, :], preferred_element_type=F32)
        acc = d if acc is None else acc + d
    h2 = h + _rms(acc, gpost_ref[...])
    gate = jax.nn.sigmoid(jnp.dot(h2.astype(BF16), wg_ref[...], preferred_element_type=F32))
    emb = jnp.dot(p_ref[...].astype(BF16), wp_ref[...], preferred_element_type=F32)
    o_ref[...] = h2 + gate * emb


def _mlp(layer, h, p, gpre, wup, wdn, gpost, wg, wp):
    T = h.shape[0]
    row = lambda i: (i, 0)
    return pl.pallas_call(
        _mlp_kernel,
        out_shape=jax.ShapeDtypeStruct((T, D_MODEL), F32),
        grid=(T // TM,),
        in_specs=[
            pl.BlockSpec((TM, D_MODEL), row),
            pl.BlockSpec((None, TM, PLE_DIM), lambda i: (layer, i, 0)),
            _const_spec((1, D_MODEL)),
            _layer_spec(layer, (D_MODEL, D_FF)),
            _layer_spec(layer, (D_FF, D_MODEL)),
            _const_spec((1, D_MODEL)),
            _layer_spec(layer, (D_MODEL, D_MODEL)),
            _layer_spec(layer, (PLE_DIM, D_MODEL)),
        ],
        out_specs=pl.BlockSpec((TM, D_MODEL), row),
        compiler_params=pltpu.CompilerParams(
            dimension_semantics=("parallel",), vmem_limit_bytes=VMEM_LIMIT_BYTES),
        name="mlp_ple",
    )(h, p, gpre, wup, wdn, gpost, wg, wp)


def _lane_constants():
    inv_freq = ROPE_THETA ** (-jnp.arange(0, ROPE_DIM, 2, dtype=F32) / ROPE_DIM)
    lane = np.arange(LANES) % HEAD_DIM
    rot = lane < ROPE_DIM
    freq = jnp.where(jnp.asarray(rot), inv_freq[lane % ROPE_HALF], 0.0)
    lo = jnp.asarray(np.where(lane < ROPE_HALF, -1.0, 0.0), F32)
    hi = jnp.asarray(np.where(rot & (lane >= ROPE_HALF), 1.0, 0.0), F32)
    return jnp.concatenate([jnp.stack([freq, lo, hi]), jnp.zeros((5, LANES), F32)], axis=0)


def kernel(x, p, positions, norm_mix_pre, w_in, conv_dw, conv_b, conv_ln_g, conv_ln_b, conv_pw,
           pool_w, pool_scale, w_out, norm_mix_post, norm_mlp_pre, w_up, w_down, norm_mlp_post,
           ple_proj, ple_gate):
    batch, seq, d_model = x.shape
    depth = w_in.shape[0]
    assert d_model == D_MODEL and seq % TQ == 0 and seq % TM == 0 and TQ == KT and batch % SEQ_TILE == 0
    T = batch * seq

    w_in_pad = jnp.concatenate(
        [w_in[:, :, :IN_UNPADDED_SPLIT],
         jnp.zeros((depth, D_MODEL, COL_CONV - IN_UNPADDED_SPLIT), w_in.dtype),
         w_in[:, :, IN_UNPADDED_SPLIT:]], axis=2).astype(BF16)
    dw_pad = jnp.concatenate([conv_dw, jnp.zeros((depth, HALO - CONV_KERNEL, CONV_WIDTH), F32)], axis=1)
    pool_bd = jnp.einsum('lgcd,gh->lgchd', pool_w, jnp.eye(POOL_GROUPS, dtype=F32)).reshape(
        depth, POOL_WIDTH, POOL_WIDTH).astype(BF16)
    conv_pw_b = conv_pw.astype(BF16)
    w_out_b = w_out.astype(BF16)
    w_up_b = w_up.astype(BF16)
    w_down_b = w_down.astype(BF16)
    gate_b = ple_gate.astype(BF16)
    proj_b = ple_proj.astype(BF16)
    vec = lambda a, i: a[i].reshape(1, -1)

    lane_consts = _lane_constants()
    pos = positions.reshape(T, 1)
    h = x.reshape(T, D_MODEL)
    p2 = p.reshape(depth, T, PLE_DIM)
    for i in range(depth):
        q, k, v, qi, ki, wi, uc, up = _inproj(i, h, pos, vec(norm_mix_pre, i), w_in_pad, lane_consts)
        y = _attention(qi, wi, q, ki, k, v, batch, seq)
        h = _mix(i, y, uc, up, h, dw_pad[i], vec(conv_b, i), vec(conv_ln_g, i), vec(conv_ln_b, i),
                 conv_pw_b, pool_bd, vec(pool_scale, i), w_out_b, vec(norm_mix_post, i), batch, seq)
        h = _mlp(i, h, p2, vec(norm_mlp_pre, i), w_up_b, w_down_b, vec(norm_mlp_post, i),
                 gate_b, proj_b)
    return h.reshape(batch, seq, D_MODEL)
```

```python
import collections
import functools

import numpy as np
import jax
import jax.numpy as jnp
from jax import lax
from jax.experimental import pallas as pl
from jax.experimental.pallas import tpu as pltpu

F32 = jnp.float32
BF16 = jnp.bfloat16
I32 = jnp.int32

D_MODEL = 1024
HEAD_DIM = 64
ATTN_WIDTH = 512
ATTN_HEADS = 8
KV_HEADS = 2
GROUP = ATTN_HEADS // KV_HEADS
IDX_HEADS = 16
IDX_DIM = 64
INDEX_TOPK = 256
CONV_WIDTH = 256
CONV_KERNEL = 31
POOL_WIDTH = 256
POOL_GROUPS = 4
POOL_GROUP_DIM = 64
POOL_WINDOWS = (2, 4, 8, 16)
D_FF = 4096
PLE_DIM = 256
ROPE_THETA = 500000.0
ROPE_DIM = 16
ROPE_HALF = ROPE_DIM // 2
NORM_EPS = 1e-6
LN_EPS = 1e-5
ATT_SCALE = HEAD_DIM ** -0.5
Q_SCALE = ATT_SCALE * float(np.log2(np.e))
IDX_SCALE = (IDX_DIM ** -0.5) * (IDX_HEADS ** -0.5)

LANES = 128
SUBLANES = 8
VMEM_LIMIT_BYTES = 56 * 1024 * 1024

COL_Q = 0
COL_K = COL_Q + ATTN_WIDTH
COL_V = COL_K + KV_HEADS * HEAD_DIM
COL_QI = COL_V + KV_HEADS * HEAD_DIM
COL_KIWI = COL_QI + IDX_HEADS * IDX_DIM
COL_CONV = COL_KIWI + LANES
COL_POOL = COL_CONV + 2 * CONV_WIDTH
IN_PAD_WIDTH = COL_POOL + POOL_WIDTH
IN_UNPADDED_SPLIT = ATTN_WIDTH + 2 * KV_HEADS * HEAD_DIM + IDX_HEADS * IDX_DIM + IDX_DIM + IDX_HEADS
WI_LANE0 = IDX_DIM

TM = 512
TQ = 256
KT = 256
KH = 128
SEQ_TILE = 4
HALO = 32
SEARCH_UNROLL = 4
COUNT_CHAINS = 8
SEARCH_ROUNDS = 12
VT_ROWS = HEAD_DIM + 16

NEG_MASK = -1e30
M_INIT = -2e30
INT_MIN = -2 ** 31


def _const_spec(shape):
    zeros = (0,) * len(shape)
    return pl.BlockSpec(shape, lambda *_: zeros, pipeline_mode=pl.Buffered(1))


def _layer_spec(layer, shape):
    index = (layer,) + (0,) * len(shape)
    return pl.BlockSpec((None,) + tuple(shape), lambda *_: index, pipeline_mode=pl.Buffered(1))


def _rms(x, gain):
    return x * lax.rsqrt(jnp.mean(x * x, axis=-1, keepdims=True) + NORM_EPS) * gain


def _inproj_kernel(h_ref, pos_ref, g_ref, w_ref, lc_ref,
                   q_ref, k_ref, v_ref, qi_ref, ki_ref, wi_ref, uc_ref, up_ref):
    a = _rms(h_ref[...], g_ref[...]).astype(BF16)
    pos = pos_ref[...].astype(F32)
    ang = pos * lc_ref[0:1, :]
    cos = jnp.cos(ang)
    sin = jnp.sin(ang)
    s_lo = sin * lc_ref[1:2, :]
    s_hi = sin * lc_ref[2:3, :]

    def rope(x):
        return x * cos + pltpu.roll(x, LANES - ROPE_HALF, 1) * s_lo + pltpu.roll(x, ROPE_HALF, 1) * s_hi

    def proj(c0, c1):
        return jnp.dot(a, w_ref[:, c0:c1], preferred_element_type=F32)

    u = proj(COL_Q, COL_K)
    for s in range(ATTN_WIDTH // LANES):
        sl = slice(s * LANES, (s + 1) * LANES)
        q_ref[:, sl] = (rope(u[:, sl]) * Q_SCALE).astype(BF16)
    u = proj(COL_K, COL_QI)
    k_ref[...] = rope(u[:, :LANES]).astype(BF16)
    v_ref[...] = u[:, LANES:].astype(BF16)
    u = proj(COL_QI, COL_KIWI)
    for s in range(IDX_HEADS * IDX_DIM // LANES):
        sl = slice(s * LANES, (s + 1) * LANES)
        qi_ref[:, sl] = rope(u[:, sl]).astype(BF16)
    u = proj(COL_KIWI, COL_CONV)
    ki_ref[...] = rope(u).astype(BF16)
    wi_ref[...] = u * IDX_SCALE
    uc_ref[...] = proj(COL_CONV, COL_POOL)
    up_ref[...] = proj(COL_POOL, IN_PAD_WIDTH)


def _inproj(layer, h, pos, gain, w, lane_consts):
    T = h.shape[0]
    row = lambda i: (i, 0)
    out_shape = (
        jax.ShapeDtypeStruct((T, ATTN_WIDTH), BF16),
        jax.ShapeDtypeStruct((T, LANES), BF16),
        jax.ShapeDtypeStruct((T, LANES), BF16),
        jax.ShapeDtypeStruct((T, IDX_HEADS * IDX_DIM), BF16),
        jax.ShapeDtypeStruct((T, LANES), BF16),
        jax.ShapeDtypeStruct((T, LANES), F32),
        jax.ShapeDtypeStruct((T, 2 * CONV_WIDTH), F32),
        jax.ShapeDtypeStruct((T, POOL_WIDTH), F32),
    )
    return pl.pallas_call(
        _inproj_kernel,
        out_shape=out_shape,
        grid=(T // TM,),
        in_specs=[
            pl.BlockSpec((TM, D_MODEL), row),
            pl.BlockSpec((TM, 1), row),
            _const_spec((1, D_MODEL)),
            _layer_spec(layer, (D_MODEL, IN_PAD_WIDTH)),
            _const_spec((8, LANES)),
        ],
        out_specs=tuple(pl.BlockSpec((TM, s.shape[1]), row) for s in out_shape),
        compiler_params=pltpu.CompilerParams(
            dimension_semantics=("parallel",), vmem_limit_bytes=VMEM_LIMIT_BYTES),
        name="inproj",
    )(h, pos, gain, w, lane_consts)


_NT = (((1,), (1,)), ((), ()))

_Seq = collections.namedtuple(
    "_Seq", "qi wi q ki k v y idx lo qis qs vt m l o")


def _key_rows(kt):
    return pl.ds(pl.multiple_of(kt * KT, KT), KT)


def _sortable(x):
    bits = pltpu.bitcast(x, I32)
    return bits ^ ((bits >> 31) & 0x7FFFFFFF)


def _indexer(nkt, q_start, seqs):
    w_ts = []
    for sq in seqs:
        for hh in range(IDX_HEADS):
            sq.qis[hh] = sq.qi[:, hh * IDX_DIM:(hh + 1) * IDX_DIM]
        w_ts.append(sq.wi[...].T)

    def key_tile(kt, carry):
        out = []
        for sq, w_t, (cmax, cmin) in zip(seqs, w_ts, carry):
            for half in range(KT // KH):
                rows = pl.ds(pl.multiple_of(kt * KT + half * KH, KH), KH)
                ki = sq.ki[rows, 0:IDX_DIM]
                acc = None
                for hh in range(IDX_HEADS):
                    d = lax.dot_general(ki, sq.qis[hh], _NT, preferred_element_type=F32)
                    term = jnp.maximum(d, 0.0) * w_t[WI_LANE0 + hh:WI_LANE0 + hh + 1, :]
                    acc = term if acc is None else acc + term
                kpos = lax.broadcasted_iota(I32, (KH, TQ), 0) + (kt * KT + half * KH)
                qpos = lax.broadcasted_iota(I32, (KH, TQ), 1) + q_start
                blk = jnp.where(kpos <= qpos, acc, -jnp.inf)
                sq.idx[rows, :] = blk
                cmax = jnp.maximum(cmax, jnp.max(blk, axis=0, keepdims=True))
                cmin = jnp.minimum(
                    cmin, jnp.min(jnp.where(blk > -jnp.inf, blk, jnp.inf), axis=0, keepdims=True))
            out.append((cmax, cmin))
        return tuple(out)

    row = lambda v: jnp.full((1, TQ), v, F32)
    return lax.fori_loop(0, nkt, key_tile, tuple((row(-jnp.inf), row(jnp.inf)) for _ in seqs))


def _select(nkt, k_sel, q_start, seqs, stats):
    kf = float(k_sel)

    def count_ge(ts):
        tbs = [jnp.broadcast_to(t, (SUBLANES, TQ)) for t in ts]

        def add_rows(start, n_rows, accs):
            out = []
            for sq, tb, acc in zip(seqs, tbs, accs):
                tiles = sq.idx.at[pl.ds(start, n_rows), :]
                chains = [None] * COUNT_CHAINS
                for r in range(n_rows // SUBLANES):
                    hit = jnp.where(tiles[r * SUBLANES:(r + 1) * SUBLANES, :] >= tb, 1.0, 0.0)
                    c = r % COUNT_CHAINS
                    chains[c] = hit if chains[c] is None else chains[c] + hit
                while len(chains) > 1:
                    chains = [chains[a] + chains[a + 1] for a in range(0, len(chains), 2)]
                out.append(acc + chains[0])
            return tuple(out)

        accs = lax.fori_loop(
            0, nkt // 2, lambda i, a: add_rows(pl.multiple_of(i * (2 * KT), 2 * KT), 2 * KT, a),
            tuple(jnp.zeros((SUBLANES, TQ), F32) for _ in seqs))
        accs = lax.cond(nkt % 2 == 1,
                        lambda a: add_rows(pl.multiple_of((nkt - 1) * KT, KT), KT, a), lambda a: a, accs)
        return [jnp.sum(acc, axis=0, keepdims=True) for acc in accs]

    def step(states):
        mids = [lo + (hi - lo) * 0.5 for lo, hi, _ in states]
        out = []
        for (lo, hi, cnt_lo), mid, cnt in zip(states, mids, count_ge(mids)):
            up = cnt >= kf
            out.append((jnp.where(up, mid, lo), jnp.where(up, hi, mid), jnp.where(up, cnt, cnt_lo)))
        return out

    def unsettled(cnt_lo):
        return jnp.max(jnp.where(cnt_lo != kf, 1.0, 0.0))

    n = len(seqs)

    def cond(c):
        pending = c[1]
        for f in c[2:n + 1]:
            pending = jnp.maximum(pending, f)
        return (c[0] < SEARCH_ROUNDS) & (pending > 0.0)

    def body(c):
        states = [c[n + 1 + 3 * s:n + 4 + 3 * s] for s in range(n)]
        for _ in range(SEARCH_UNROLL):
            states = step(states)
        return (c[0] + 1,) + tuple(unsettled(st[2]) for st in states) + tuple(x for st in states for x in st)

    n_causal = (lax.broadcasted_iota(I32, (1, TQ), 1) + (q_start + 1)).astype(F32)
    init = [(cmin, cmax, n_causal) for cmax, cmin in stats]
    out = lax.while_loop(cond, body, (jnp.int32(0),) + tuple(unsettled(st[2]) for st in init)
                         + tuple(x for st in init for x in st))
    for s, sq in enumerate(seqs):
        sq.lo[...] = out[n + 1 + 3 * s]
    return out[1:n + 1]


def _select_exact(nkt, k_sel, sq):
    def over_tiles(fn, init):
        return lax.fori_loop(0, nkt, lambda kt, c: fn(sq.idx[_key_rows(kt), :], c), init)

    def count(pred):
        return over_tiles(
            lambda x, c: c + jnp.sum(pred(_sortable(x)).astype(I32), axis=0, keepdims=True),
            jnp.zeros((1, TQ), I32))

    def search(i, prefix):
        cand = prefix | lax.shift_left(jnp.int32(1), 31 - i)
        return jnp.where(count(lambda key: key >= (cand ^ INT_MIN)) >= k_sel, cand, prefix)

    thr = lax.fori_loop(0, 32, search, jnp.zeros((1, TQ), I32)) ^ INT_MIN
    need = (k_sel - count(lambda key: key > thr)).astype(F32)
    r = lax.broadcasted_iota(I32, (KT, KT), 0)
    c = lax.broadcasted_iota(I32, (KT, KT), 1)
    earlier = (c < r).astype(BF16)

    def mark(kt, carry):
        x = sq.idx[_key_rows(kt), :]
        key = _sortable(x)
        eq = key == thr
        eqf = jnp.where(eq, 1.0, 0.0)
        rank = jnp.dot(earlier, eqf.astype(BF16), preferred_element_type=F32) + carry
        keep = ((key > thr) | (eq & (rank < need))) & (x > -jnp.inf)
        sq.idx[_key_rows(kt), :] = jnp.where(keep, 1.0, -jnp.inf)
        return carry + jnp.sum(eqf, axis=0, keepdims=True)

    lax.fori_loop(0, nkt, mark, jnp.zeros((1, TQ), F32))
    sq.lo[...] = jnp.full((1, TQ), 0.5, F32)


def _attend(nkt, seqs):
    for sq in seqs:
        for hh in range(ATTN_HEADS):
            g, jj = divmod(hh, GROUP)
            sq.qs[g, jj * TQ:(jj + 1) * TQ, :] = sq.q[:, hh * HEAD_DIM:(hh + 1) * HEAD_DIM]
        sq.m[...] = jnp.full(sq.m.shape, M_INIT, F32)
        sq.l[...] = jnp.zeros(sq.l.shape, F32)
        sq.o[...] = jnp.zeros(sq.o.shape, F32)
    los = [sq.lo[...] for sq in seqs]

    def key_tile(kt, carry):
        rows = _key_rows(kt)
        all_scores = [[lax.dot_general(sq.k[rows, g * HEAD_DIM:(g + 1) * HEAD_DIM], sq.qs[g], _NT,
                                       preferred_element_type=F32).astype(BF16) for g in range(KV_HEADS)]
                      for sq in seqs]
        for sq, lo, scores in zip(seqs, los, all_scores):
            bias = jnp.where(sq.idx[rows, :] >= lo, 0.0, NEG_MASK).astype(BF16)
            for g in range(KV_HEADS):
                probs, alphas = [], []
                for jj in range(GROUP):
                    hh = g * GROUP + jj
                    s = scores[g][:, jj * TQ:(jj + 1) * TQ] + bias
                    m_old = sq.m[hh]
                    m_new = jnp.maximum(m_old, jnp.max(s, axis=0, keepdims=True).astype(F32))
                    alphas.append(jnp.exp2(m_old - m_new))
                    probs.append(jnp.exp2(s - m_new.astype(BF16)))
                    sq.m[hh] = m_new
                pv = jnp.dot(sq.vt[kt, g * VT_ROWS:(g + 1) * VT_ROWS, :], jnp.concatenate(probs, axis=1),
                             preferred_element_type=F32)
                for jj in range(GROUP):
                    hh = g * GROUP + jj
                    cols = slice(jj * TQ, (jj + 1) * TQ)
                    orow = slice(hh * HEAD_DIM, (hh + 1) * HEAD_DIM)
                    sq.o[orow, :] = sq.o[orow, :] * alphas[jj] + pv[0:HEAD_DIM, cols]
                    sq.l[hh] = sq.l[hh] * alphas[jj] + pv[HEAD_DIM:HEAD_DIM + 1, cols]
        return carry

    lax.fori_loop(0, nkt, key_tile, 0)
    for sq in seqs:
        for hh in range(ATTN_HEADS):
            orow = slice(hh * HEAD_DIM, (hh + 1) * HEAD_DIM)
            sq.o[orow, :] = sq.o[orow, :] / sq.l[hh]
        sq.y[...] = sq.o[...].T.astype(BF16)


def _attn_kernel(seq, k_sel, *refs):
    seqs = [_Seq(*(r.at[s] for r in refs)) for s in range(SEQ_TILE)]
    j = pl.program_id(1)
    nkt = j + 1
    q_start = j * TQ

    @pl.when(j == 0)
    def _():
        for sq in seqs:
            for kt in range(seq // KT):
                v_t = sq.v[kt * KT:(kt + 1) * KT, :].astype(F32).T.astype(BF16)
                for g in range(KV_HEADS):
                    sq.vt[kt, g * VT_ROWS:g * VT_ROWS + HEAD_DIM, :] = v_t[g * HEAD_DIM:(g + 1) * HEAD_DIM, :]
                    sq.vt[kt, g * VT_ROWS + HEAD_DIM:(g + 1) * VT_ROWS, :] = jnp.ones(
                        (VT_ROWS - HEAD_DIM, KT), BF16)

    all_selected = (j + 1) * TQ <= k_sel

    @pl.when(all_selected)
    def _():
        def fill(kt, carry):
            kpos = lax.broadcasted_iota(I32, (KT, TQ), 0) + kt * KT
            qpos = lax.broadcasted_iota(I32, (KT, TQ), 1) + q_start
            for sq in seqs:
                sq.idx[_key_rows(kt), :] = jnp.where(kpos <= qpos, 1.0, -jnp.inf)
            return carry
        lax.fori_loop(0, nkt, fill, 0)
        for sq in seqs:
            sq.lo[...] = jnp.full((1, TQ), 0.5, F32)

    @pl.when(jnp.logical_not(all_selected))
    def _():
        stats = _indexer(nkt, q_start, seqs)
        failed = _select(nkt, k_sel, q_start, seqs, stats)
        for sq, f in zip(seqs, failed):
            @pl.when(f > 0.0)
            def _(sq=sq):
                _select_exact(nkt, k_sel, sq)

    _attend(nkt, seqs)


def _attention(qi, wi, q, ki, k, v, batch, seq):
    nq = seq // TQ
    k_sel = min(INDEX_TOPK, seq // 4)
    per_seq = lambda a: a.reshape(batch, seq, a.shape[-1])
    tile = lambda b, j: (b, j, 0)
    whole = lambda b, j: (b, 0, 0)
    scratch = lambda shape, dtype: pltpu.VMEM((SEQ_TILE,) + shape, dtype)
    y = pl.pallas_call(
        functools.partial(_attn_kernel, seq, k_sel),
        out_shape=jax.ShapeDtypeStruct((batch, seq, ATTN_WIDTH), BF16),
        grid=(batch // SEQ_TILE, nq),
        in_specs=[
            pl.BlockSpec((SEQ_TILE, TQ, IDX_HEADS * IDX_DIM), tile),
            pl.BlockSpec((SEQ_TILE, TQ, LANES), tile),
            pl.BlockSpec((SEQ_TILE, TQ, ATTN_WIDTH), tile),
            pl.BlockSpec((SEQ_TILE, seq, LANES), whole),
            pl.BlockSpec((SEQ_TILE, seq, LANES), whole),
            pl.BlockSpec((SEQ_TILE, seq, LANES), whole),
        ],
        out_specs=pl.BlockSpec((SEQ_TILE, TQ, ATTN_WIDTH), tile),
        scratch_shapes=[
            scratch((seq, TQ), F32),
            scratch((1, TQ), F32),
            scratch((IDX_HEADS, TQ, IDX_DIM), BF16),
            scratch((KV_HEADS, GROUP * TQ, HEAD_DIM), BF16),
            scratch((seq // KT, KV_HEADS * VT_ROWS, KT), BF16),
            scratch((ATTN_HEADS, 1, TQ), F32),
            scratch((ATTN_HEADS, 1, TQ), F32),
            scratch((ATTN_WIDTH, TQ), F32),
        ],
        compiler_params=pltpu.CompilerParams(
            dimension_semantics=("arbitrary", "arbitrary"), vmem_limit_bytes=VMEM_LIMIT_BYTES),
        name="dsa_attention",
    )(per_seq(qi), per_seq(wi), per_seq(q), per_seq(ki), per_seq(k), per_seq(v))
    return y.reshape(batch * seq, ATTN_WIDTH)


def _mix_kernel(y_ref, uc_ref, uch_ref, up_ref, uph_ref, h_ref, dw_ref, cb_ref, lg_ref, lb_ref,
                pw_ref, plw_ref, psc_ref, wo_ref, gpost_ref, o_ref, ypad_ref, upad_ref):
    j = pl.program_id(1)
    has_prev = (j > 0).astype(F32)

    def glu(u):
        return u[:, :CONV_WIDTH] * jax.nn.sigmoid(u[:, CONV_WIDTH:])

    def fill_shifted(ref, halo_rows, tile_rows):
        ref[0, 0:HALO, :] = halo_rows
        ref[0, HALO:HALO + TM, :] = tile_rows
        for s in range(1, SUBLANES):
            ref[s, 0:HALO + TM - SUBLANES, :] = ref[0, s:s + HALO + TM - SUBLANES, :]

    def rows_from(ref, off):
        s = off % SUBLANES
        return ref[s, off - s:off - s + TM, :]

    fill_shifted(ypad_ref, glu(uch_ref[...]) * has_prev, glu(uc_ref[...]))
    acc = None
    for tap in range(CONV_KERNEL):
        term = rows_from(ypad_ref, HALO - (CONV_KERNEL - 1) + tap) * dw_ref[tap:tap + 1, :]
        acc = term if acc is None else acc + term
    yc = acc + cb_ref[...]
    mu = jnp.mean(yc, axis=-1, keepdims=True)
    var = jnp.mean(jnp.square(yc - mu), axis=-1, keepdims=True)
    yn = (yc - mu) * lax.rsqrt(var + LN_EPS) * lg_ref[...] + lb_ref[...]
    ys = yn * jax.nn.sigmoid(yn)
    y_conv = jnp.dot(ys.astype(BF16), pw_ref[...], preferred_element_type=F32)

    u0 = up_ref[...]
    end = HALO + TM
    upad_ref[0, 0:HALO, :] = uph_ref[...] * has_prev
    upad_ref[0, HALO:end, :] = u0
    sums = []
    for level, w in enumerate(POOL_WINDOWS, start=1):
        assert w == 2 ** level and SUBLANES * level <= HALO
        start, back = SUBLANES * level, w // 2
        doubled = upad_ref[level - 1, start:end, :] + upad_ref[level - 1, start - back:end - back, :]
        if level < len(POOL_WINDOWS):
            upad_ref[level, start:end, :] = doubled
        sums.append(doubled[HALO - start:, :])
    lane = lax.broadcasted_iota(I32, (TM, POOL_WIDTH), 1)
    win_sum = sums[-1]
    win = jnp.full((TM, POOL_WIDTH), POOL_WINDOWS[-1], I32)
    for gi in range(POOL_GROUPS - 2, -1, -1):
        in_group = lane < (gi + 1) * POOL_GROUP_DIM
        win_sum = jnp.where(in_group, sums[gi], win_sum)
        win = jnp.where(in_group, POOL_WINDOWS[gi], win)
    t = lax.broadcasted_iota(I32, (TM, POOL_WIDTH), 0) + j * TM
    count = jnp.minimum(t + 1, win).astype(F32)
    pooled = win_sum / count - u0
    y_pool = jnp.dot(pooled.astype(BF16), plw_ref[...], preferred_element_type=F32) * psc_ref[...]

    c0, c1 = ATTN_WIDTH, ATTN_WIDTH + CONV_WIDTH
    mix = (jnp.dot(y_ref[...], wo_ref[0:c0, :], preferred_element_type=F32)
           + jnp.dot(y_conv.astype(BF16), wo_ref[c0:c1, :], preferred_element_type=F32)
           + jnp.dot(y_pool.astype(BF16), wo_ref[c1:, :], preferred_element_type=F32))
    o_ref[...] = h_ref[...] + _rms(mix, gpost_ref[...])


def _mix(layer, y, uc, up, h, dw, cb, lg, lb, pw, plw, psc, wo, gpost, batch, seq):
    T = batch * seq
    nt = seq // TM
    tile = lambda b, j: (b * nt + j, 0)
    halo_per_tile = TM // HALO
    halo = lambda b, j: (b * (seq // HALO) + jnp.maximum(j * halo_per_tile - 1, 0), 0)
    return pl.pallas_call(
        _mix_kernel,
        out_shape=jax.ShapeDtypeStruct((T, D_MODEL), F32),
        grid=(batch, nt),
        in_specs=[
            pl.BlockSpec((TM, ATTN_WIDTH), tile),
            pl.BlockSpec((TM, 2 * CONV_WIDTH), tile),
            pl.BlockSpec((HALO, 2 * CONV_WIDTH), halo),
            pl.BlockSpec((TM, POOL_WIDTH), tile),
            pl.BlockSpec((HALO, POOL_WIDTH), halo),
            pl.BlockSpec((TM, D_MODEL), tile),
            _const_spec((HALO, CONV_WIDTH)),
            _const_spec((1, CONV_WIDTH)),
            _const_spec((1, CONV_WIDTH)),
            _const_spec((1, CONV_WIDTH)),
            _layer_spec(layer, (CONV_WIDTH, CONV_WIDTH)),
            _layer_spec(layer, (POOL_WIDTH, POOL_WIDTH)),
            _const_spec((1, POOL_WIDTH)),
            _layer_spec(layer, (D_MODEL, D_MODEL)),
            _const_spec((1, D_MODEL)),
        ],
        out_specs=pl.BlockSpec((TM, D_MODEL), tile),
        scratch_shapes=[
            pltpu.VMEM((SUBLANES, HALO + TM, CONV_WIDTH), F32),
            pltpu.VMEM((len(POOL_WINDOWS), HALO + TM, POOL_WIDTH), F32),
        ],
        compiler_params=pltpu.CompilerParams(
            dimension_semantics=("parallel", "parallel"), vmem_limit_bytes=VMEM_LIMIT_BYTES),
        name="mix_out",
    )(y, uc, uc, up, up, h, dw, cb, lg, lb, pw, plw, psc, wo, gpost)


FF_CHUNK = 1024


def _mlp_kernel(h_ref, p_ref, gpre_ref, wup_ref, wdn_ref, gpost_ref, wg_ref, wp_ref, o_ref):
    h = h_ref[...]
    m = _rms(h, gpre_ref[...]).astype(BF16)
    acc = None
    for c in range(D_FF // FF_CHUNK):
        cs = slice(c * FF_CHUNK, (c + 1) * FF_CHUNK)
        r = jnp.maximum(jnp.dot(m, wup_ref[:, cs], preferred_element_type=F32), 0.0)
        d = jnp.dot((r * r).astype(BF16), wdn_ref[cs, :], preferred_element_type=F32)
        acc = d if acc is None else acc + d
    h2 = h + _rms(acc, gpost_ref[...])
    gate = jax.nn.sigmoid(jnp.dot(h2.astype(BF16), wg_ref[...], preferred_element_type=F32))
    emb = jnp.dot(p_ref[...].astype(BF16), wp_ref[...], preferred_element_type=F32)
    o_ref[...] = h2 + gate * emb


def _mlp(layer, h, p, gpre, wup, wdn, gpost, wg, wp):
    T = h.shape[0]
    row = lambda i: (i, 0)
    return pl.pallas_call(
        _mlp_kernel,
        out_shape=jax.ShapeDtypeStruct((T, D_MODEL), F32),
        grid=(T // TM,),
        in_specs=[
            pl.BlockSpec((TM, D_MODEL), row),
            pl.BlockSpec((None, TM, PLE_DIM), lambda i: (layer, i, 0)),
            _const_spec((1, D_MODEL)),
            _layer_spec(layer, (D_MODEL, D_FF)),
            _layer_spec(layer, (D_FF, D_MODEL)),
            _const_spec((1, D_MODEL)),
            _layer_spec(layer, (D_MODEL, D_MODEL)),
            _layer_spec(layer, (PLE_DIM, D_MODEL)),
        ],
        out_specs=pl.BlockSpec((TM, D_MODEL), row),
        compiler_params=pltpu.CompilerParams(
            dimension_semantics=("parallel",), vmem_limit_bytes=VMEM_LIMIT_BYTES),
        name="mlp_ple",
    )(h, p, gpre, wup, wdn, gpost, wg, wp)


def _lane_constants():
    inv_freq = ROPE_THETA ** (-jnp.arange(0, ROPE_DIM, 2, dtype=F32) / ROPE_DIM)
    lane = np.arange(LANES) % HEAD_DIM
    rot = lane < ROPE_DIM
    freq = jnp.where(jnp.asarray(rot), inv_freq[lane % ROPE_HALF], 0.0)
    lo = jnp.asarray(np.where(lane < ROPE_HALF, -1.0, 0.0), F32)
    hi = jnp.asarray(np.where(rot & (lane >= ROPE_HALF), 1.0, 0.0), F32)
    return jnp.concatenate([jnp.stack([freq, lo, hi]), jnp.zeros((5, LANES), F32)], axis=0)


def kernel(x, p, positions, norm_mix_pre, w_in, conv_dw, conv_b, conv_ln_g, conv_ln_b, conv_pw,
           pool_w, pool_scale, w_out, norm_mix_post, norm_mlp_pre, w_up, w_down, norm_mlp_post,
           ple_proj, ple_gate):
    batch, seq, d_model = x.shape
    depth = w_in.shape[0]
    assert d_model == D_MODEL and seq % TQ == 0 and seq % TM == 0 and TQ == KT and batch % SEQ_TILE == 0
    T = batch * seq

    w_in_pad = jnp.concatenate(
        [w_in[:, :, :IN_UNPADDED_SPLIT],
         jnp.zeros((depth, D_MODEL, COL_CONV - IN_UNPADDED_SPLIT), w_in.dtype),
         w_in[:, :, IN_UNPADDED_SPLIT:]], axis=2).astype(BF16)
    dw_pad = jnp.concatenate([conv_dw, jnp.zeros((depth, HALO - CONV_KERNEL, CONV_WIDTH), F32)], axis=1)
    pool_bd = jnp.einsum('lgcd,gh->lgchd', pool_w, jnp.eye(POOL_GROUPS, dtype=F32)).reshape(
        depth, POOL_WIDTH, POOL_WIDTH).astype(BF16)
    conv_pw_b = conv_pw.astype(BF16)
    w_out_b = w_out.astype(BF16)
    w_up_b = w_up.astype(BF16)
    w_down_b = w_down.astype(BF16)
    gate_b = ple_gate.astype(BF16)
    proj_b = ple_proj.astype(BF16)
    vec = lambda a, i: a[i].reshape(1, -1)

    lane_consts = _lane_constants()
    pos = positions.reshape(T, 1)
    h = x.reshape(T, D_MODEL)
    p2 = p.reshape(depth, T, PLE_DIM)
    for i in range(depth):
        q, k, v, qi, ki, wi, uc, up = _inproj(i, h, pos, vec(norm_mix_pre, i), w_in_pad, lane_consts)
        y = _attention(qi, wi, q, ki, k, v, batch, seq)
        h = _mix(i, y, uc, up, h, dw_pad[i], vec(conv_b, i), vec(conv_ln_g, i), vec(conv_ln_b, i),
                 conv_pw_b, pool_bd, vec(pool_scale, i), w_out_b, vec(norm_mix_post, i), batch, seq)
        h = _mlp(i, h, p2, vec(norm_mlp_pre, i), w_up_b, w_down_b, vec(norm_mlp_post, i),
                 gate_b, proj_b)
    return h.reshape(batch, seq, D_MODEL)
```
